```python
import math
import jax, jax.numpy as jnp
from jax import lax
import numpy as np

D_MODEL = 1024
BATCH = 1
SEQ = 16384
DEPTH = 2
DEC_BATCH = 32
DEC_SEQ = 1
PAST_LEN = 16384
PAGE_SIZE = 128

N_ATT_HEADS = 8
N_KV_HEADS = 2
ATT_HEAD_DIM = D_MODEL // N_ATT_HEADS
ATT_WIDTH = N_ATT_HEADS * ATT_HEAD_DIM
KV_WIDTH = N_KV_HEADS * ATT_HEAD_DIM
Q_PER_KV = N_ATT_HEADS // N_KV_HEADS
MOBA_BLOCK = 256
MOBA_TOPK = 3
MOBA_QUERY_ROWS = 64
SSD_WIDTH = D_MODEL
SSD_HEAD_DIM = 64
N_SSD_HEADS = SSD_WIDTH // SSD_HEAD_DIM
SSD_GROUPS = 2
D_STATE = 128
CONV_WIDTH = 4
SSD_CHUNK = 256
CONV_DIM = SSD_WIDTH + 2 * SSD_GROUPS * D_STATE
MIX_WIDTH = ATT_WIDTH + SSD_WIDTH
Q_END = ATT_WIDTH
K_END = Q_END + KV_WIDTH
V_END = K_END + KV_WIDTH
G_END = V_END + ATT_WIDTH
Z_END = G_END + SSD_WIDTH
XBC_END = Z_END + CONV_DIM
IN_WIDTH = XBC_END + N_SSD_HEADS
NORM_EPS = 1e-5

kernel_name = 'hymba_moba_ssd_decode_step'


def rmsnorm(x, g):
    xf = x.astype(jnp.float32)
    xf = xf * lax.rsqrt(jnp.mean(xf * xf, axis=-1, keepdims=True) + NORM_EPS)
    return (xf * g.astype(jnp.float32)).astype(x.dtype)


def moba_attention(q, k, v, q_pos):
    bsz, n_q, n_h, d = q.shape
    n_blk = k.shape[1] // MOBA_BLOCK
    kb = k.reshape(bsz, n_blk, MOBA_BLOCK, N_KV_HEADS, d).transpose(0, 3, 1, 2, 4)
    vb = v.reshape(bsz, n_blk, MOBA_BLOCK, N_KV_HEADS, d).transpose(0, 3, 1, 2, 4)
    kbar = jnp.repeat(jnp.mean(kb.astype(jnp.float32), axis=3), Q_PER_KV, axis=1)
    k_sel = min(MOBA_TOPK, n_blk)
    kv_head = jnp.arange(n_h) // Q_PER_KV
    b_idx = jnp.arange(bsz)[:, None, None, None]
    h_idx = kv_head[None, None, :, None]
    scale = ATT_HEAD_DIM ** -0.5
    rows = max(1, min(n_q, MOBA_QUERY_ROWS // bsz))
    n_steps = -(-n_q // rows)
    pad = n_steps * rows - n_q
    q_p = jnp.pad(q, ((0, 0), (0, pad), (0, 0), (0, 0)))
    pos_p = jnp.pad(q_pos, (0, pad), mode='edge')
    q_blocks = q_p.reshape(bsz, n_steps, rows, n_h, d).transpose(1, 0, 2, 3, 4)
    pos_blocks = pos_p.reshape(n_steps, rows)

    def one_block(args):
        qq, pp = args
        own = pp // MOBA_BLOCK
        gate = jnp.einsum('bqhd,bhnd->bqhn', qq.astype(jnp.float32), kbar)
        past = jnp.arange(n_blk)[None, :] < own[:, None]
        gate = jnp.where(past[None, :, None, :], gate, -jnp.inf)
        _, top_idx = lax.top_k(gate, k_sel)
        own_idx = jnp.broadcast_to(own[None, :, None, None], top_idx.shape[:3] + (1,))
        idx = jnp.concatenate([top_idx, own_idx], axis=-1)
        kg = kb[b_idx, h_idx, idx]
        vg = vb[b_idx, h_idx, idx]
        rank_ok = jnp.concatenate([jnp.arange(k_sel)[None, :] < own[:, None],
                                   jnp.ones((rows, 1), bool)], axis=-1)
        key_pos = idx[..., None] * MOBA_BLOCK + jnp.arange(MOBA_BLOCK)
        mask = rank_ok[None, :, None, :, None] & (key_pos <= pp[None, :, None, None, None])
        logits = jnp.einsum('bqhd,bqhjkd->bqhjk', qq, kg,
                            preferred_element_type=jnp.float32) * scale
        logits = jnp.where(mask, logits, -jnp.inf)
        probs = jax.nn.softmax(logits.reshape(logits.shape[:3] + (-1,)), axis=-1).reshape(logits.shape)
        out = jnp.einsum('bqhjk,bqhjkd->bqhd', probs, vg.astype(jnp.float32))
        return out.astype(q.dtype)

    out = lax.map(one_block, (q_blocks, pos_blocks))
    out = out.transpose(1, 0, 2, 3, 4).reshape(bsz, n_steps * rows, n_h, d)
    return out[:, :n_q]


def segsum(a):
    t = a.shape[-1]
    a_rep = jnp.broadcast_to(a[..., None], a.shape + (t,))
    a_rep = jnp.where(jnp.tril(jnp.ones((t, t), bool), -1), a_rep, 0.0)
    s = jnp.cumsum(a_rep, axis=-2)
    return jnp.where(jnp.tril(jnp.ones((t, t), bool)), s, -jnp.inf)


def ssd_chunked(x, dt, a_coef, bh, ch):
    bsz, seq, n_h, p = x.shape
    n = bh.shape[-1]
    t = SSD_CHUNK
    pad = (-seq) % t
    padt = lambda z: jnp.pad(z, ((0, 0), (0, pad)) + ((0, 0),) * (z.ndim - 2))
    x, dt, bh, ch = padt(x), padt(dt), padt(bh), padt(ch)
    nc = (seq + pad) // t
    xdt = (x * dt[..., None]).reshape(bsz, nc, t, n_h, p)
    a = (dt * a_coef).reshape(bsz, nc, t, n_h).transpose(0, 3, 1, 2)
    bc = bh.reshape(bsz, nc, t, n_h, n)
    cc = ch.reshape(bsz, nc, t, n_h, n)
    a_cum = jnp.cumsum(a, axis=-1)
    scores = jnp.einsum('bclhn,bcshn->bhcls', cc, bc) * jnp.exp(segsum(a))
    y_diag = jnp.einsum('bhcls,bcshp->bclhp', scores, xdt)
    decay = jnp.exp(a_cum[..., -1:] - a_cum)
    states = jnp.einsum('bclhn,bhcl,bclhp->bchpn', bc, decay, xdt)
    states = jnp.concatenate([jnp.zeros_like(states[:, :1]), states], axis=1)
    chunk_decay = jnp.exp(segsum(jnp.pad(a_cum[..., -1], ((0, 0), (0, 0), (1, 0)))))
    states = jnp.einsum('bhzc,bchpn->bzhpn', chunk_decay, states)
    y_off = jnp.einsum('bclhn,bchpn,bhcl->bclhp', cc, states[:, :-1], jnp.exp(a_cum))
    y = (y_diag + y_off).reshape(bsz, nc * t, n_h, p)[:, :seq]
    return y, states[:, -1]


def ssd_recurrent(x, dt, a_coef, bh, ch, h0):
    def step(h, inp):
        x_t, dt_t, b_t, c_t = inp
        h = h * jnp.exp(dt_t * a_coef)[..., None, None] + jnp.einsum('bhp,bhn->bhpn', x_t * dt_t[..., None], b_t)
        return h, jnp.einsum('bhpn,bhn->bhp', h, c_t)
    xs = (jnp.moveaxis(x, 1, 0), jnp.moveaxis(dt, 1, 0), jnp.moveaxis(bh, 1, 0), jnp.moveaxis(ch, 1, 0))
    h, ys = lax.scan(step, h0, xs)
    return jnp.moveaxis(ys, 0, 1), h


def causal_dwconv(x_ext, w, b):
    c = x_ext.shape[-1]
    out = lax.conv_general_dilated(x_ext, w[:, None, :].astype(x_ext.dtype), window_strides=(1,),
                                   padding='VALID', dimension_numbers=('NWC', 'WIO', 'NWC'),
                                   feature_group_count=c)
    return jax.nn.silu(out + b.astype(x_ext.dtype))


def mixer_layer(x, start, k_past, v_past, conv_hist, ssm_h0,
                norm_g, w_in, conv_w, conv_b, dt_bias, a_log, d_skip, ssd_norm_g, w_out):
    bsz, seq, _ = x.shape
    u = jnp.matmul(rmsnorm(x, norm_g), w_in)
    q, k, v, g_att, z, xbc, dt_raw = jnp.split(u, [Q_END, K_END, V_END, G_END, Z_END, XBC_END], axis=-1)
    q = q.reshape(bsz, seq, N_ATT_HEADS, ATT_HEAD_DIM)
    k = k.reshape(bsz, seq, N_KV_HEADS, ATT_HEAD_DIM)
    v = v.reshape(bsz, seq, N_KV_HEADS, ATT_HEAD_DIM)
    k_all = k if k_past is None else jnp.concatenate([k_past.astype(k.dtype), k], axis=1)
    v_all = v if v_past is None else jnp.concatenate([v_past.astype(v.dtype), v], axis=1)
    kpad = (-k_all.shape[1]) % MOBA_BLOCK
    k_all = jnp.pad(k_all, ((0, 0), (0, kpad), (0, 0), (0, 0)))
    v_all = jnp.pad(v_all, ((0, 0), (0, kpad), (0, 0), (0, 0)))
    q_pos = start + jnp.arange(seq, dtype=jnp.int32)
    o_att = moba_attention(q, k_all, v_all, q_pos).reshape(bsz, seq, ATT_WIDTH) * jax.nn.silu(g_att)
    if conv_hist is None:
        conv_hist = jnp.zeros((bsz, CONV_WIDTH - 1, CONV_DIM), xbc.dtype)
    xbc_ext = jnp.concatenate([conv_hist.astype(xbc.dtype), xbc], axis=1)
    new_conv = xbc_ext[:, -(CONV_WIDTH - 1):]
    xbc = causal_dwconv(xbc_ext, conv_w, conv_b)
    xs, bm, cm = jnp.split(xbc, [SSD_WIDTH, SSD_WIDTH + SSD_GROUPS * D_STATE], axis=-1)
    xs = xs.reshape(bsz, seq, N_SSD_HEADS, SSD_HEAD_DIM).astype(jnp.float32)
    rep = N_SSD_HEADS // SSD_GROUPS
    bh = jnp.repeat(bm.reshape(bsz, seq, SSD_GROUPS, D_STATE).astype(jnp.float32), rep, axis=2)
    ch = jnp.repeat(cm.reshape(bsz, seq, SSD_GROUPS, D_STATE).astype(jnp.float32), rep, axis=2)
    dt = jax.nn.softplus(dt_raw.astype(jnp.float32) + dt_bias.astype(jnp.float32))
    a_coef = -jnp.exp(a_log.astype(jnp.float32))
    if ssm_h0 is None:
        y, h_new = ssd_chunked(xs, dt, a_coef, bh, ch)
    else:
        y, h_new = ssd_recurrent(xs, dt, a_coef, bh, ch, ssm_h0.astype(jnp.float32))
    y = y + d_skip.astype(jnp.float32)[:, None] * xs
    y = y.reshape(bsz, seq, SSD_WIDTH).astype(x.dtype) * jax.nn.silu(z)
    o_ssd = rmsnorm(y, ssd_norm_g)
    out = jnp.matmul(jnp.concatenate([o_att, o_ssd], axis=-1), w_out)
    return x + out, k, v, new_conv, h_new.astype(x.dtype)


def setup_inputs(seed: int = 0) -> dict:
    key = jax.random.key(seed)
    ks = jax.random.split(key, 20)
    f32 = jnp.float32
    n_pages = PAST_LEN // PAGE_SIZE
    n_used = DEC_BATCH * n_pages
    n_phys = n_used + n_used // 4
    x_prompt = jax.random.normal(ks[0], (BATCH, SEQ, D_MODEL), f32)
    x_sample = jax.random.normal(ks[1], (DEC_BATCH, DEC_SEQ, D_MODEL), f32)
    cache_k = jax.random.normal(ks[2], (DEPTH, n_phys, PAGE_SIZE, N_KV_HEADS, ATT_HEAD_DIM), f32)
    cache_v = jax.random.normal(ks[3], (DEPTH, n_phys, PAGE_SIZE, N_KV_HEADS, ATT_HEAD_DIM), f32)
    state_conv = jax.random.normal(ks[4], (DEPTH, DEC_BATCH, CONV_WIDTH - 1, CONV_DIM), f32)
    state_ssm = 0.1 * jax.random.normal(ks[5], (DEPTH, DEC_BATCH, N_SSD_HEADS, SSD_HEAD_DIM, D_STATE), f32)
    page_table = jax.random.permutation(ks[6], n_phys)[:n_used].reshape(DEC_BATCH, n_pages).astype(jnp.int32)
    norm_g = 1.0 + 0.02 * jax.random.normal(ks[7], (DEPTH, D_MODEL), f32)
    w_in = jax.random.normal(ks[8], (DEPTH, D_MODEL, IN_WIDTH), f32) * D_MODEL ** -0.5
    conv_w = jax.random.normal(ks[9], (DEPTH, CONV_WIDTH, CONV_DIM), f32) * CONV_WIDTH ** -0.5
    conv_b = 0.01 * jax.random.normal(ks[10], (DEPTH, CONV_DIM), f32)
    dt0 = jnp.exp(jax.random.uniform(ks[11], (DEPTH, N_SSD_HEADS), f32, math.log(1e-3), math.log(1e-1)))
    dt_bias = dt0 + jnp.log(-jnp.expm1(-dt0))
    a_log = jnp.log(jax.random.uniform(ks[12], (DEPTH, N_SSD_HEADS), f32, 1.0, 16.0))
    d_skip = 1.0 + 0.1 * jax.random.normal(ks[13], (DEPTH, N_SSD_HEADS), f32)
    ssd_norm_g = 1.0 + 0.02 * jax.random.normal(ks[14], (DEPTH, SSD_WIDTH), f32)
    w_out = jax.random.normal(ks[15], (DEPTH, MIX_WIDTH, D_MODEL), f32) * MIX_WIDTH ** -0.5
    final_norm_g = 1.0 + 0.02 * jax.random.normal(ks[16], (D_MODEL,), f32)
    return {'x_prompt': x_prompt, 'x_sample': x_sample, 'cache_k': cache_k, 'cache_v': cache_v,
            'state_conv': state_conv, 'state_ssm': state_ssm, 'page_table': page_table,
            'norm_g': norm_g, 'w_in': w_in, 'conv_w': conv_w, 'conv_b': conv_b, 'dt_bias': dt_bias,
            'a_log': a_log, 'd_skip': d_skip, 'ssd_norm_g': ssd_norm_g, 'w_out': w_out,
            'final_norm_g': final_norm_g}


def reference(x_prompt, x_sample, cache_k, cache_v, state_conv, state_ssm, page_table,
              norm_g, w_in, conv_w, conv_b, dt_bias, a_log, d_skip, ssd_norm_g, w_out, final_norm_g):
    n_dec, n_pages = page_table.shape
    past_len = n_pages * PAGE_SIZE
    hp, hs = x_prompt, x_sample
    kp_l, vp_l, cp_l, sp_l = [], [], [], []
    ks_l, vs_l, cs_l, ss_l = [], [], [], []
    for l in range(DEPTH):
        params = (norm_g[l], w_in[l], conv_w[l], conv_b[l], dt_bias[l], a_log[l], d_skip[l],
                  ssd_norm_g[l], w_out[l])
        hp, kp, vp, cp, sp = mixer_layer(hp, 0, None, None, None, None, *params)
        k_past = cache_k[l, page_table].reshape(n_dec, past_len, N_KV_HEADS, ATT_HEAD_DIM)
        v_past = cache_v[l, page_table].reshape(n_dec, past_len, N_KV_HEADS, ATT_HEAD_DIM)
        hs, ks, vs, cs, ss = mixer_layer(hs, past_len, k_past, v_past, state_conv[l], state_ssm[l], *params)
        kp_l.append(kp); vp_l.append(vp); cp_l.append(cp); sp_l.append(sp)
        ks_l.append(ks); vs_l.append(vs); cs_l.append(cs); ss_l.append(ss)
    y_prompt = rmsnorm(hp, final_norm_g)
    y_sample = rmsnorm(hs, final_norm_g)
    return (y_prompt, y_sample,
            jnp.stack(kp_l), jnp.stack(vp_l), jnp.stack(cp_l), jnp.stack(sp_l),
            jnp.stack(ks_l), jnp.stack(vs_l), jnp.stack(cs_l), jnp.stack(ss_l))
```

```python
import functools

import jax
import jax.numpy as jnp
from jax import lax
from jax.experimental import pallas as pl
from jax.experimental.pallas import tpu as pltpu

F32 = jnp.float32
BF16 = jnp.bfloat16

D_MODEL = 1024
N_ATT_HEADS = 8
N_KV_HEADS = 2
HEAD_DIM = 128
Q_PER_KV = N_ATT_HEADS // N_KV_HEADS
ATT_WIDTH = N_ATT_HEADS * HEAD_DIM
KV_WIDTH = N_KV_HEADS * HEAD_DIM
MOBA_BLOCK = 256
MOBA_TOPK = 3
PAGE_SIZE = 128
SSD_WIDTH = 1024
SSD_HEAD_DIM = 64
N_SSD_HEADS = SSD_WIDTH // SSD_HEAD_DIM
SSD_GROUPS = 2
HEADS_PER_GROUP = N_SSD_HEADS // SSD_GROUPS
GROUP_WIDTH = HEADS_PER_GROUP * SSD_HEAD_DIM
D_STATE = 128
CONV_WIDTH = 4
SSD_CHUNK = 256
CONV_DIM = SSD_WIDTH + 2 * SSD_GROUPS * D_STATE
MIX_WIDTH = ATT_WIDTH + SSD_WIDTH
Q_END = ATT_WIDTH
K_END = Q_END + KV_WIDTH
V_END = K_END + KV_WIDTH
G_END = V_END + ATT_WIDTH
Z_END = G_END + SSD_WIDTH
XBC_END = Z_END + CONV_DIM
IN_WIDTH = XBC_END + N_SSD_HEADS
NORM_EPS = 1e-5
ATT_SCALE = HEAD_DIM ** -0.5

LANES = 128
SUBLANES = 8
VMEM_LIMIT_BYTES = 56 * 1024 * 1024

MASK_BIAS = -1e30

NT_DIMS = (((1,), (1,)), ((), ()))
TN_DIMS = (((0,), (0,)), ((), ()))


def _cparams(*semantics):
    return pltpu.CompilerParams(dimension_semantics=semantics, vmem_limit_bytes=VMEM_LIMIT_BYTES)


def _resident(shape):
    nd = len(shape)
    return pl.BlockSpec(shape, lambda *_: (0,) * nd, pipeline_mode=pl.Buffered(1))


def _rmsnorm(x, g):
    return x * lax.rsqrt(jnp.mean(x * x, axis=-1, keepdims=True) + NORM_EPS) * g


def _silu(x):
    return x * jax.nn.sigmoid(x)


def _split3(x):
    hi = x.astype(BF16)
    r = x - hi.astype(F32)
    mid = r.astype(BF16)
    lo = (r - mid.astype(F32)).astype(BF16)
    return hi, mid, lo


def _dot3(parts, m, dims=None):
    if dims is None:
        f = lambda p: jnp.dot(p, m, preferred_element_type=F32)
    else:
        f = lambda p: lax.dot_general(p, m, dims, preferred_element_type=F32)
    return (f(parts[2]) + f(parts[1])) + f(parts[0])


def _in_proj_kernel(x_ref, g_ref, wqT_ref, wkv_ref, wvT_ref, wrest_ref, wdt_ref,
                    qT_ref, k_ref, v_ref, ka_ref, vT_ref, gz_ref, xbc_ref, dt_ref, ksum_ref):
    tm = x_ref.shape[0]
    xn = _rmsnorm(x_ref[...], g_ref[...]).astype(BF16)
    qT_ref[...] = lax.dot_general(wqT_ref[...], xn, NT_DIMS, preferred_element_type=F32).astype(BF16)
    vT = lax.dot_general(wvT_ref[...], xn, NT_DIMS, preferred_element_type=F32).astype(BF16)
    kv = jnp.dot(xn, wkv_ref[...], preferred_element_type=F32)
    k = kv[:, :KV_WIDTH]
    k_ref[...] = k
    v_ref[...] = kv[:, KV_WIDTH:]
    blk0 = pl.program_id(0) * (tm // MOBA_BLOCK)
    lane = lax.broadcasted_iota(jnp.int32, (MOBA_BLOCK, LANES), 1)
    for b in range(tm // MOBA_BLOCK):
        rows = slice(b * MOBA_BLOCK, (b + 1) * MOBA_BLOCK)
        onehot = (lane == blk0 + b).astype(BF16)
        for h in range(N_KV_HEADS):
            ka_ref[h, b, :, :HEAD_DIM] = k[rows, h * HEAD_DIM:(h + 1) * HEAD_DIM].astype(BF16)
            ka_ref[h, b, :, HEAD_DIM:] = onehot
        vT_ref[b] = vT[:, rows]
        ksum_ref[0, b:b + 1, :] = jnp.sum(k[rows], axis=0, keepdims=True)
    rest = jnp.dot(xn, wrest_ref[...], preferred_element_type=F32)
    gz_ref[...] = rest[:, :ATT_WIDTH + SSD_WIDTH]
    xbc_ref[...] = rest[:, ATT_WIDTH + SSD_WIDTH:]
    dt_ref[...] = jnp.dot(xn, wdt_ref[...], preferred_element_type=F32)


def _in_proj_weights(w_in):
    wb = w_in.astype(BF16)
    wqT = wb[:, :Q_END].T
    wkv = wb[:, Q_END:V_END]
    wvT = wb[:, K_END:V_END].T
    wrest = wb[:, V_END:XBC_END]
    wdt = jnp.pad(wb[:, XBC_END:], ((0, 0), (0, LANES - N_SSD_HEADS)))
    return wqT, wkv, wvT, wrest, wdt


def _in_proj(x, norm_g, weights, tm):
    n = x.shape[0]
    assert n % tm == 0 and tm % MOBA_BLOCK == 0
    wqT, wkv, wvT, wrest, wdt = weights
    nb = tm // MOBA_BLOCK
    row = lambda i: (i, 0)
    out_shape = (
        jax.ShapeDtypeStruct((ATT_WIDTH, n), BF16),
        jax.ShapeDtypeStruct((n, KV_WIDTH), F32),
        jax.ShapeDtypeStruct((n, KV_WIDTH), F32),
        jax.ShapeDtypeStruct((N_KV_HEADS, n // MOBA_BLOCK, MOBA_BLOCK, 2 * HEAD_DIM), BF16),
        jax.ShapeDtypeStruct((n // MOBA_BLOCK, KV_WIDTH, MOBA_BLOCK), BF16),
        jax.ShapeDtypeStruct((n, ATT_WIDTH + SSD_WIDTH), F32),
        jax.ShapeDtypeStruct((n, CONV_DIM), F32),
        jax.ShapeDtypeStruct((n, LANES), F32),
        jax.ShapeDtypeStruct((n // tm, nb, KV_WIDTH), F32),
    )
    out_specs = (
        pl.BlockSpec((ATT_WIDTH, tm), lambda i: (0, i)),
        pl.BlockSpec((tm, KV_WIDTH), row),
        pl.BlockSpec((tm, KV_WIDTH), row),
        pl.BlockSpec((N_KV_HEADS, nb, MOBA_BLOCK, 2 * HEAD_DIM), lambda i: (0, i, 0, 0)),
        pl.BlockSpec((nb, KV_WIDTH, MOBA_BLOCK), lambda i: (i, 0, 0)),
        pl.BlockSpec((tm, ATT_WIDTH + SSD_WIDTH), row),
        pl.BlockSpec((tm, CONV_DIM), row),
        pl.BlockSpec((tm, LANES), row),
        pl.BlockSpec((1, nb, KV_WIDTH), lambda i: (i, 0, 0)),
    )
    in_specs = [pl.BlockSpec((tm, D_MODEL), row), _resident((1, D_MODEL)), _resident(wqT.shape),
                _resident(wkv.shape), _resident(wvT.shape), _resident(wrest.shape), _resident(wdt.shape)]
    return pl.pallas_call(
        _in_proj_kernel, out_shape=out_shape, grid=(n // tm,), in_specs=in_specs, out_specs=out_specs,
        compiler_params=_cparams("parallel"), name="in_proj",
    )(x, norm_g.reshape(1, D_MODEL), wqT, wkv, wvT, wrest, wdt)


def _moba_prompt_kernel(qT_ref, ka_ref, vT_ref, ksum_ref, g_ref, o_ref, qa_ref):
    i = pl.program_id(1)
    n_blk = ksum_ref.shape[0]
    kbar = (ksum_ref[...] * (1.0 / MOBA_BLOCK)).astype(BF16)
    blk = lax.broadcasted_iota(jnp.int32, (n_blk, MOBA_BLOCK), 0)
    key_pos = lax.broadcasted_iota(jnp.int32, (MOBA_BLOCK, MOBA_BLOCK), 0)
    qry_pos = lax.broadcasted_iota(jnp.int32, (MOBA_BLOCK, MOBA_BLOCK), 1)

    for h in range(Q_PER_KV):
        cols = slice(h * MOBA_BLOCK, (h + 1) * MOBA_BLOCK)
        qT = qT_ref[h * HEAD_DIM:(h + 1) * HEAD_DIM, :]
        gate = jnp.dot(kbar, qT, preferred_element_type=F32)
        gate = jnp.where(blk < i, gate, -jnp.inf)
        sel = blk == i
        for r in range(MOBA_TOPK):
            best = jnp.max(gate, axis=0, keepdims=True)
            first = jnp.min(jnp.where(gate == best, blk, n_blk), axis=0, keepdims=True)
            pick = jnp.logical_and(blk == first, r < i)
            sel = jnp.logical_or(sel, pick)
            gate = jnp.where(pick, -jnp.inf, gate)
        qa_ref[:HEAD_DIM, cols] = qT
        qa_ref[HEAD_DIM:HEAD_DIM + n_blk, cols] = jnp.where(sel, 0.0, MASK_BIAS).astype(BF16)
        if n_blk < LANES:
            qa_ref[HEAD_DIM + n_blk:, cols] = jnp.zeros((LANES - n_blk, MOBA_BLOCK), BF16)

    for h in range(Q_PER_KV):
        cols = slice(h * MOBA_BLOCK, (h + 1) * MOBA_BLOCK)
        qa = qa_ref[:, cols]
        s = jnp.dot(ka_ref[i], qa, preferred_element_type=F32)
        s = jnp.where(key_pos <= qry_pos, s, MASK_BIAS)
        m = jnp.max(s, axis=0, keepdims=True)
        p = jnp.exp((s - m) * ATT_SCALE)
        l = jnp.sum(p, axis=0, keepdims=True)
        acc = jnp.dot(vT_ref[i], p.astype(BF16), preferred_element_type=F32)

        def past_block(j, carry):
            m, l, acc = carry
            s = jnp.dot(ka_ref[j], qa, preferred_element_type=F32)
            m_new = jnp.maximum(m, jnp.max(s, axis=0, keepdims=True))
            alpha = jnp.exp((m - m_new) * ATT_SCALE)
            p = jnp.exp((s - m_new) * ATT_SCALE)
            l = alpha * l + jnp.sum(p, axis=0, keepdims=True)
            acc = alpha * acc + jnp.dot(vT_ref[j], p.astype(BF16), preferred_element_type=F32)
            return m_new, l, acc

        m, l, acc = lax.fori_loop(0, i, past_block, (m, l, acc))
        out = (acc / l).T
        dcols = slice(h * HEAD_DIM, (h + 1) * HEAD_DIM)
        o_ref[:, dcols] = (out * _silu(g_ref[:, dcols])).astype(BF16)


def _moba_prompt(qT, ka, vT, ksum, gz):
    n = qT.shape[1]
    n_blk = n // MOBA_BLOCK
    assert n_blk <= LANES
    kv_rows = Q_PER_KV * HEAD_DIM
    return pl.pallas_call(
        _moba_prompt_kernel,
        out_shape=jax.ShapeDtypeStruct((n, ATT_WIDTH), BF16),
        grid=(N_KV_HEADS, n_blk),
        in_specs=[
            pl.BlockSpec((kv_rows, MOBA_BLOCK), lambda h, i: (h, i)),
            pl.BlockSpec((None, n_blk, MOBA_BLOCK, 2 * HEAD_DIM), lambda h, i: (h, 0, 0, 0)),
            pl.BlockSpec((n_blk, HEAD_DIM, MOBA_BLOCK), lambda h, i: (0, h, 0)),
            pl.BlockSpec((n_blk, HEAD_DIM), lambda h, i: (0, h)),
            pl.BlockSpec((MOBA_BLOCK, kv_rows), lambda h, i: (i, h)),
        ],
        out_specs=pl.BlockSpec((MOBA_BLOCK, kv_rows), lambda h, i: (i, h)),
        scratch_shapes=[pltpu.VMEM((2 * HEAD_DIM, Q_PER_KV * MOBA_BLOCK), BF16)],
        compiler_params=_cparams("parallel", "parallel"), name="moba_prompt",
    )(qT, ka, vT, ksum, gz)


def _ssd_prompt_kernel(xbc_ref, z_ref, dt_ref, convw_ref, convb_ref, dtb_ref, acoef_ref, dskip_ref, ng_ref,
                       o_ref, state_ref, xpad_ref, st_ref, y_ref):
    c = pl.program_id(0)
    t = SSD_CHUNK

    @pl.when(c == 0)
    def _():
        xpad_ref[:SUBLANES, :] = jnp.zeros((SUBLANES, CONV_DIM), F32)
        st_ref[...] = jnp.zeros_like(st_ref)

    xpad_ref[SUBLANES:, :] = xbc_ref[...]
    conv = convb_ref[...] + convw_ref[CONV_WIDTH - 1:CONV_WIDTH, :] * xpad_ref[SUBLANES:, :]
    for w in range(CONV_WIDTH - 1):
        back = CONV_WIDTH - 1 - w
        conv = conv + convw_ref[w:w + 1, :] * xpad_ref[SUBLANES - back:SUBLANES - back + t, :]
    xpad_ref[:SUBLANES, :] = xpad_ref[t:t + SUBLANES, :]
    xc = _silu(conv)
    xs = xc[:, :SSD_WIDTH]

    dt = jax.nn.softplus(dt_ref[...] + dtb_ref[...])
    a = dt * acoef_ref[...]
    row = lax.broadcasted_iota(jnp.int32, (t, t), 0)
    colm = lax.broadcasted_iota(jnp.int32, (t, t), 1)
    lower = row >= colm
    tril = lower.astype(BF16)
    a_hi, a_mid, a_lo = _split3(a)
    a_cum = ((jnp.dot(tril, a_lo, preferred_element_type=F32) + jnp.dot(tril, a_mid, preferred_element_type=F32))
             + jnp.dot(tril, a_hi, preferred_element_type=F32))
    a_cum_t = a_cum.T
    a_end = a_cum[t - 1:t, :]

    for g in range(SSD_GROUPS):
        b_g = xc[:, SSD_WIDTH + g * D_STATE:SSD_WIDTH + (g + 1) * D_STATE]
        c_g = xc[:, SSD_WIDTH + (SSD_GROUPS + g) * D_STATE:SSD_WIDTH + (SSD_GROUPS + g + 1) * D_STATE]
        b_bf = b_g.astype(BF16)
        c_bf = c_g.astype(BF16)
        cb = lax.dot_general(c_bf, b_bf, NT_DIMS, preferred_element_type=F32)
        for hh in range(HEADS_PER_GROUP):
            h = g * HEADS_PER_GROUP + hh
            hc = slice(h * SSD_HEAD_DIM, (h + 1) * SSD_HEAD_DIM)
            acol = a_cum[:, h:h + 1]
            arow = a_cum_t[h:h + 1, :]
            xs_h = xs[:, hc]
            xdt = xs_h * dt[:, h:h + 1]
            decay = jnp.exp(jnp.where(lower, acol - arow, -jnp.inf))
            y = jnp.dot((cb * decay).astype(BF16), xdt.astype(BF16), preferred_element_type=F32)
            st = st_ref[h]
            y = y + lax.dot_general(c_bf, st.astype(BF16), NT_DIMS, preferred_element_type=F32) * jnp.exp(acol)
            xdtd = xdt * jnp.exp(a_end[:, h:h + 1] - acol)
            st_ref[h] = st * jnp.exp(a_end[:, h:h + 1]) + lax.dot_general(
                xdtd.astype(BF16), b_bf, TN_DIMS, preferred_element_type=F32)
            y_ref[:, hc] = y + dskip_ref[:, h:h + 1] * xs_h

    y = y_ref[...] * _silu(z_ref[...])
    o_ref[...] = _rmsnorm(y, ng_ref[...]).astype(o_ref.dtype)

    @pl.when(c == pl.num_programs(0) - 1)
    def _():
        state_ref[...] = st_ref[...]


def _ssd_prompt(xbc, gz, dt_raw, conv_w, conv_b, dt_bias, a_log, d_skip, ssd_norm_g):
    n = xbc.shape[0]
    t = SSD_CHUNK
    assert n % t == 0
    pad = lambda v: jnp.pad(v.reshape(1, N_SSD_HEADS), ((0, 0), (0, LANES - N_SSD_HEADS)))
    a_coef = -jnp.exp(a_log.astype(F32))
    row = lambda c: (c, 0)
    return pl.pallas_call(
        _ssd_prompt_kernel,
        out_shape=(jax.ShapeDtypeStruct((n, SSD_WIDTH), BF16),
                   jax.ShapeDtypeStruct((N_SSD_HEADS, SSD_HEAD_DIM, D_STATE), F32)),
        grid=(n // t,),
        in_specs=[pl.BlockSpec((t, CONV_DIM), row),
                  pl.BlockSpec((t, SSD_WIDTH), lambda c: (c, 1)),
                  pl.BlockSpec((t, LANES), row),
                  _resident((CONV_WIDTH, CONV_DIM)), _resident((1, CONV_DIM)), _resident((1, LANES)),
                  _resident((1, LANES)), _resident((1, LANES)), _resident((1, SSD_WIDTH))],
        out_specs=(pl.BlockSpec((t, SSD_WIDTH), row),
                   pl.BlockSpec((N_SSD_HEADS, SSD_HEAD_DIM, D_STATE), lambda c: (0, 0, 0))),
        scratch_shapes=[pltpu.VMEM((t + SUBLANES, CONV_DIM), F32),
                        pltpu.VMEM((N_SSD_HEADS, SSD_HEAD_DIM, D_STATE), F32),
                        pltpu.VMEM((t, SSD_WIDTH), F32)],
        compiler_params=_cparams("arbitrary"), name="ssd_prompt",
    )(xbc, gz, dt_raw, conv_w, conv_b.reshape(1, CONV_DIM), pad(dt_bias), pad(a_coef), pad(d_skip),
      ssd_norm_g.reshape(1, SSD_WIDTH))


def _out_proj_kernel(x_ref, oa_ref, os_ref, w_ref, fg_ref, h_ref, *, final_norm):
    out = jnp.dot(oa_ref[...], w_ref[:ATT_WIDTH, :], preferred_element_type=F32)
    out = out + jnp.dot(os_ref[...], w_ref[ATT_WIDTH:, :], preferred_element_type=F32)
    h = x_ref[...] + out
    h_ref[...] = _rmsnorm(h, fg_ref[...]) if final_norm else h


def _out_proj(x, o_att, o_ssd, w_out_bf, final_g, final_norm, tm):
    n = x.shape[0]
    assert n % tm == 0
    row = lambda i: (i, 0)
    return pl.pallas_call(
        functools.partial(_out_proj_kernel, final_norm=final_norm),
        out_shape=jax.ShapeDtypeStruct((n, D_MODEL), F32),
        grid=(n // tm,),
        in_specs=[pl.BlockSpec((tm, D_MODEL), row), pl.BlockSpec((tm, ATT_WIDTH), row),
                  pl.BlockSpec((tm, SSD_WIDTH), row), _resident((MIX_WIDTH, D_MODEL)), _resident((1, D_MODEL))],
        out_specs=pl.BlockSpec((tm, D_MODEL), row),
        compiler_params=_cparams("parallel"), name="out_proj",
    )(x, o_att, o_ssd, w_out_bf, final_g.reshape(1, D_MODEL))


def _prompt_layer(x, norm_g, w_in, conv_w, conv_b, dt_bias, a_log, d_skip, ssd_norm_g, w_out, final_g,
                  final_norm, tm):
    n = x.shape[0]
    qT, k, v, ka, vT, gz, xbc, dt_raw, ksum = _in_proj(x, norm_g, _in_proj_weights(w_in), tm)
    o_att = _moba_prompt(qT, ka, vT, ksum.reshape(n // MOBA_BLOCK, KV_WIDTH), gz)
    o_ssd, state = _ssd_prompt(xbc, gz, dt_raw, conv_w, conv_b, dt_bias, a_log, d_skip, ssd_norm_g)
    h = _out_proj(x, o_att, o_ssd, w_out.astype(BF16), final_g, final_norm, tm)
    return h, k, v, xbc[n - (CONV_WIDTH - 1):], state


PAGES_PER_BLOCK = MOBA_BLOCK // PAGE_SIZE
PAGE_ROWS = PAGE_SIZE * N_KV_HEADS
SAMPLE_U_WIDTH = 49 * LANES
KBAR_GROUP = 8


def _in_proj_sample_kernel(x_ref, g_ref, w_ref, u_ref):
    xn = _rmsnorm(x_ref[...], g_ref[...]).astype(BF16)
    u_ref[...] = jnp.dot(xn, w_ref[...], preferred_element_type=F32)


def _in_proj_sample(x, norm_g, w_in):
    n = x.shape[0]
    wb = w_in.astype(BF16)
    w = jnp.concatenate([wb, jnp.repeat(wb[:, XBC_END:], SSD_HEAD_DIM, axis=1)], axis=1)
    w = jnp.pad(w, ((0, 0), (0, SAMPLE_U_WIDTH - w.shape[1])))
    return pl.pallas_call(
        _in_proj_sample_kernel, out_shape=jax.ShapeDtypeStruct((n, SAMPLE_U_WIDTH), F32), grid=(1,),
        in_specs=[_resident((n, D_MODEL)), _resident((1, D_MODEL)), _resident(w.shape)],
        out_specs=pl.BlockSpec((n, SAMPLE_U_WIDTH), lambda i: (0, 0)),
        compiler_params=_cparams("arbitrary"), name="in_proj_sample",
    )(x, norm_g.reshape(1, D_MODEL), w)


def _kbar_sample_kernel(pt_ref, cache_ref, out_ref, buf_ref, sem_ref, *, layer):
    b = pl.program_id(0)
    n_groups = pt_ref.shape[1] // KBAR_GROUP

    def copies(g, slot):
        return [pltpu.make_async_copy(cache_ref.at[layer, pt_ref[b, g * KBAR_GROUP + p]],
                                      buf_ref.at[slot, p], sem_ref.at[slot]) for p in range(KBAR_GROUP)]

    for cp in copies(0, 0):
        cp.start()
    for g in range(n_groups):
        slot = g % 2
        if g + 1 < n_groups:
            for cp in copies(g + 1, 1 - slot):
                cp.start()
        for cp in copies(g, slot):
            cp.wait()
        for bb in range(KBAR_GROUP // PAGES_PER_BLOCK):
            acc = jnp.zeros((SUBLANES, HEAD_DIM), F32)
            for p in range(PAGES_PER_BLOCK):
                page = buf_ref[slot, bb * PAGES_PER_BLOCK + p]
                acc = acc + jnp.sum(page.reshape(PAGE_ROWS // SUBLANES, SUBLANES, HEAD_DIM), axis=0)
            acc = acc + pltpu.roll(acc, 2, 0)
            acc = acc + pltpu.roll(acc, 4, 0)
            blk = g * (KBAR_GROUP // PAGES_PER_BLOCK) + bb
            for h in range(N_KV_HEADS):
                out_ref[0, h, blk:blk + 1, :] = acc[h:h + 1, :] * (1.0 / MOBA_BLOCK)


def _kbar_sample(cache_pages, page_table, layer):
    n_seq, n_pages = page_table.shape
    assert n_pages % KBAR_GROUP == 0
    n_blk = n_pages // PAGES_PER_BLOCK
    return pl.pallas_call(
        functools.partial(_kbar_sample_kernel, layer=layer),
        out_shape=jax.ShapeDtypeStruct((n_seq, N_KV_HEADS, n_blk, HEAD_DIM), F32),
        grid_spec=pltpu.PrefetchScalarGridSpec(
            num_scalar_prefetch=1, grid=(n_seq,),
            in_specs=[pl.BlockSpec(memory_space=pl.ANY)],
            out_specs=pl.BlockSpec((1, N_KV_HEADS, n_blk, HEAD_DIM), lambda b, pt: (b, 0, 0, 0)),
            scratch_shapes=[pltpu.VMEM((2, KBAR_GROUP, PAGE_ROWS, HEAD_DIM), F32),
                            pltpu.SemaphoreType.DMA((2,))]),
        compiler_params=_cparams("arbitrary"), name="kbar_sample",
    )(page_table, cache_pages)


def _gate_sample_kernel(q_ref, kbar_ref, idx_ref, gate_ref):
    n_seq, _, n_blk, _ = kbar_ref.shape
    for b in range(n_seq):
        for kh in range(N_KV_HEADS):
            r0 = b * N_ATT_HEADS + kh * Q_PER_KV
            q4 = q_ref[r0:r0 + Q_PER_KV, :].astype(BF16)
            gate_ref[r0:r0 + Q_PER_KV, :] = lax.dot_general(
                q4, kbar_ref[b, kh].astype(BF16), NT_DIMS, preferred_element_type=F32)
    gate = gate_ref[...]
    lane = lax.broadcasted_iota(jnp.int32, gate.shape, 1)
    idx_ref[...] = jnp.zeros_like(idx_ref)
    for r in range(MOBA_TOPK):
        best = jnp.max(gate, axis=1, keepdims=True)
        first = jnp.min(jnp.where(gate == best, lane, n_blk), axis=1, keepdims=True)
        idx_ref[:, r:r + 1] = first
        gate = jnp.where(lane == first, -jnp.inf, gate)


def _gate_sample(q_rows, kbar):
    n_seq, _, n_blk, _ = kbar.shape
    assert n_blk >= MOBA_TOPK
    rows = n_seq * N_ATT_HEADS
    return pl.pallas_call(
        _gate_sample_kernel, out_shape=jax.ShapeDtypeStruct((rows, LANES), jnp.int32), grid=(1,),
        in_specs=[_resident((rows, HEAD_DIM)), _resident(kbar.shape)],
        out_specs=pl.BlockSpec((rows, LANES), lambda i: (0, 0)),
        scratch_shapes=[pltpu.VMEM((rows, n_blk), F32)],
        compiler_params=_cparams("arbitrary"), name="gate_sample",
    )(q_rows, kbar)


def _attn_sample_kernel(pt_ref, idx_ref, q_ref, knew_ref, vnew_ref, g_ref, ck_ref, cv_ref, o_ref,
                        kbuf_ref, vbuf_ref, sem_ref, *, layer):
    b = pl.program_id(0)
    n_sel = N_ATT_HEADS * MOBA_TOPK

    def copies():
        out = []
        for s in range(n_sel):
            blk = idx_ref[b * n_sel + s]
            for p in range(PAGES_PER_BLOCK):
                page = pt_ref[b, blk * PAGES_PER_BLOCK + p]
                slot = s * PAGES_PER_BLOCK + p
                out.append(pltpu.make_async_copy(ck_ref.at[layer, page], kbuf_ref.at[slot], sem_ref.at[0]))
                out.append(pltpu.make_async_copy(cv_ref.at[layer, page], vbuf_ref.at[slot], sem_ref.at[1]))
        return out

    cps = copies()
    for cp in cps:
        cp.start()
    for cp in cps:
        cp.wait()

    row_head = lax.broadcasted_iota(jnp.int32, (1, PAGE_ROWS), 1) % N_KV_HEADS
    for h in range(N_ATT_HEADS):
        kh = h // Q_PER_KV
        qh = q_ref[0, h:h + 1, :].astype(BF16)
        q8 = jnp.broadcast_to(qh, (SUBLANES, HEAD_DIM))
        valid = row_head == kh
        scores = []
        for s in range(MOBA_TOPK * PAGES_PER_BLOCK):
            page = kbuf_ref[h * MOBA_TOPK * PAGES_PER_BLOCK + s].astype(BF16)
            sc = lax.dot_general(q8, page, NT_DIMS, preferred_element_type=F32)[0:1, :]
            scores.append(jnp.where(valid, sc, -jnp.inf))
        k_self = knew_ref[0, kh:kh + 1, :].astype(BF16).astype(F32)
        s_self = jnp.sum(qh.astype(F32) * k_self, axis=1, keepdims=True)
        m = s_self
        for sc in scores:
            m = jnp.maximum(m, jnp.max(sc, axis=1, keepdims=True))
        p_self = jnp.exp((s_self - m) * ATT_SCALE)
        l = p_self
        acc = p_self.astype(BF16).astype(F32) * vnew_ref[0, kh:kh + 1, :].astype(BF16).astype(F32)
        for s, sc in enumerate(scores):
            p = jnp.exp((sc - m) * ATT_SCALE)
            l = l + jnp.sum(p, axis=1, keepdims=True)
            p8 = jnp.broadcast_to(p.astype(BF16), (SUBLANES, PAGE_ROWS))
            page = vbuf_ref[h * MOBA_TOPK * PAGES_PER_BLOCK + s].astype(BF16)
            acc = acc + jnp.dot(p8, page, preferred_element_type=F32)[0:1, :]
        o_ref[0, h:h + 1, :] = ((acc / l) * _silu(g_ref[0, h:h + 1, :])).astype(o_ref.dtype)


def _attn_sample(q3, knew3, vnew3, g3, cache_k_pages, cache_v_pages, page_table, idx_flat, layer):
    n_seq = q3.shape[0]
    n_slots = N_ATT_HEADS * MOBA_TOPK * PAGES_PER_BLOCK
    per_seq = lambda r: pl.BlockSpec((1, r, HEAD_DIM), lambda b, pt, ix: (b, 0, 0))
    return pl.pallas_call(
        functools.partial(_attn_sample_kernel, layer=layer),
        out_shape=jax.ShapeDtypeStruct((n_seq, N_ATT_HEADS, HEAD_DIM), BF16),
        grid_spec=pltpu.PrefetchScalarGridSpec(
            num_scalar_prefetch=2, grid=(n_seq,),
            in_specs=[per_seq(N_ATT_HEADS), per_seq(N_KV_HEADS), per_seq(N_KV_HEADS), per_seq(N_ATT_HEADS),
                      pl.BlockSpec(memory_space=pl.ANY), pl.BlockSpec(memory_space=pl.ANY)],
            out_specs=per_seq(N_ATT_HEADS),
            scratch_shapes=[pltpu.VMEM((n_slots, PAGE_ROWS, HEAD_DIM), F32),
                            pltpu.VMEM((n_slots, PAGE_ROWS, HEAD_DIM), F32),
                            pltpu.SemaphoreType.DMA((2,))]),
        compiler_params=_cparams("arbitrary"), name="attn_sample",
    )(page_table, idx_flat, q3, knew3, vnew3, g3, cache_k_pages, cache_v_pages)


def _column(row):
    return jnp.broadcast_to(row, (LANES, LANES)).T


def _ssd_sample_kernel(xbc_ref, z_ref, dt_ref, sc_ref, ss_ref, convw_ref, convb_ref, dtb_ref, acoef_ref,
                       dskip_ref, ng_ref, o_ref, nc_ref, ns_ref, y_ref):
    hist = sc_ref[0]
    new = xbc_ref[0]
    conv = convb_ref[...] + convw_ref[CONV_WIDTH - 1:CONV_WIDTH, :] * new
    for w in range(CONV_WIDTH - 1):
        conv = conv + convw_ref[w:w + 1, :] * hist[w:w + 1, :]
    nc_ref[0, :CONV_WIDTH - 2, :] = hist[1:, :]
    nc_ref[0, CONV_WIDTH - 2:, :] = new
    xc = _silu(conv)
    xs = xc[:, :SSD_WIDTH]
    dt = jax.nn.softplus(dt_ref[0] + dtb_ref[...])
    d_a = jnp.exp(dt * acoef_ref[...])
    xdt = xs * dt
    heads_per_chunk = LANES // SSD_HEAD_DIM
    for c in range(SSD_WIDTH // LANES):
        lanes = slice(c * LANES, (c + 1) * LANES)
        x_col = _column(xdt[:, lanes])
        da_col = _column(d_a[:, lanes])
        y_cols = []
        for hh in range(heads_per_chunk):
            h = c * heads_per_chunk + hh
            g = h // HEADS_PER_GROUP
            rows = slice(hh * SSD_HEAD_DIM, (hh + 1) * SSD_HEAD_DIM)
            b_g = xc[:, SSD_WIDTH + g * D_STATE:SSD_WIDTH + (g + 1) * D_STATE]
            c_g = xc[:, SSD_WIDTH + (SSD_GROUPS + g) * D_STATE:SSD_WIDTH + (SSD_GROUPS + g + 1) * D_STATE]
            st = ss_ref[0, h] * da_col[rows, :] + x_col[rows, :] * b_g
            ns_ref[0, h] = st
            y_cols.append(jnp.sum(st * c_g, axis=1, keepdims=True))
        y_col = jnp.concatenate(y_cols, axis=0)
        y_ref[:, lanes] = jnp.broadcast_to(y_col, (LANES, LANES)).T[0:1, :]
    y = (y_ref[...] + dskip_ref[...] * xs) * _silu(z_ref[0])
    o_ref[0] = _rmsnorm(y, ng_ref[...]).astype(o_ref.dtype)


def _ssd_sample(xbc3, z3, dte3, state_conv, state_ssm, layer, conv_w, conv_b, dt_bias, a_log, d_skip, ssd_norm_g):
    n_seq = xbc3.shape[0]
    per_channel = lambda v: jnp.repeat(v.astype(F32), SSD_HEAD_DIM).reshape(1, SSD_WIDTH)
    seq3 = lambda w: pl.BlockSpec((1, 1, w), lambda b: (b, 0, 0))
    return pl.pallas_call(
        _ssd_sample_kernel,
        out_shape=(jax.ShapeDtypeStruct((n_seq, 1, SSD_WIDTH), BF16),
                   jax.ShapeDtypeStruct((n_seq, CONV_WIDTH - 1, CONV_DIM), F32),
                   jax.ShapeDtypeStruct((n_seq, N_SSD_HEADS, SSD_HEAD_DIM, D_STATE), F32)),
        grid=(n_seq,),
        in_specs=[seq3(CONV_DIM), seq3(SSD_WIDTH), seq3(SSD_WIDTH),
                  pl.BlockSpec((None, 1, CONV_WIDTH - 1, CONV_DIM), lambda b: (layer, b, 0, 0)),
                  pl.BlockSpec((None, 1, N_SSD_HEADS, SSD_HEAD_DIM, D_STATE), lambda b: (layer, b, 0, 0, 0)),
                  _resident((CONV_WIDTH, CONV_DIM)), _resident((1, CONV_DIM)), _resident((1, SSD_WIDTH)),
                  _resident((1, SSD_WIDTH)), _resident((1, SSD_WIDTH)), _resident((1, SSD_WIDTH))],
        out_specs=(seq3(SSD_WIDTH),
                   pl.BlockSpec((1, CONV_WIDTH - 1, CONV_DIM), lambda b: (b, 0, 0)),
                   pl.BlockSpec((1, N_SSD_HEADS, SSD_HEAD_DIM, D_STATE), lambda b: (b, 0, 0, 0))),
        scratch_shapes=[pltpu.VMEM((1, SSD_WIDTH), F32)],
        compiler_params=_cparams("parallel"), name="ssd_sample",
    )(xbc3, z3, dte3, state_conv, state_ssm, conv_w, conv_b.reshape(1, CONV_DIM), per_channel(dt_bias),
      per_channel(-jnp.exp(a_log.astype(F32))), per_channel(d_skip), ssd_norm_g.reshape(1, SSD_WIDTH))


def _sample_layer(x, layer, cache_k_pages, cache_v_pages, state_conv, state_ssm, page_table,
                  norm_g, w_in, conv_w, conv_b, dt_bias, a_log, d_skip, ssd_norm_g, w_out, final_g, final_norm):
    n_seq = x.shape[0]
    u = _in_proj_sample(x, norm_g, w_in)
    q, k, v, g_att = u[:, :Q_END], u[:, Q_END:K_END], u[:, K_END:V_END], u[:, V_END:G_END]
    z, xbc, dte = u[:, G_END:Z_END], u[:, Z_END:XBC_END], u[:, IN_WIDTH:IN_WIDTH + SSD_WIDTH]
    kbar = _kbar_sample(cache_k_pages, page_table, layer)
    idx = _gate_sample(q.reshape(n_seq * N_ATT_HEADS, HEAD_DIM), kbar)
    heads = lambda a, r: a.reshape(n_seq, r, HEAD_DIM)
    o_att = _attn_sample(heads(q, N_ATT_HEADS), heads(k, N_KV_HEADS), heads(v, N_KV_HEADS),
                         heads(g_att, N_ATT_HEADS), cache_k_pages, cache_v_pages, page_table,
                         idx[:, :MOBA_TOPK].reshape(-1), layer)
    rows3 = lambda a: a.reshape(n_seq, 1, a.shape[1])
    o_ssd, new_conv, new_ssm = _ssd_sample(rows3(xbc), rows3(z), rows3(dte), state_conv, state_ssm, layer,
                                           conv_w, conv_b, dt_bias, a_log, d_skip, ssd_norm_g)
    h = _out_proj(x, o_att.reshape(n_seq, ATT_WIDTH), o_ssd.reshape(n_seq, SSD_WIDTH), w_out.astype(BF16),
                  final_g, final_norm, n_seq)
    return h, k, v, new_conv, new_ssm


PROMPT_ROW_TILE = 512


def kernel(x_prompt, x_sample, cache_k, cache_v, state_conv, state_ssm, page_table, norm_g, w_in, conv_w,
           conv_b, dt_bias, a_log, d_skip, ssd_norm_g, w_out, final_norm_g):
    depth = w_in.shape[0]
    batch, seq, _ = x_prompt.shape
    n_seq, dec_seq, _ = x_sample.shape
    assert batch == 1 and dec_seq == 1
    n_phys = cache_k.shape[1]
    ck = cache_k.reshape(depth, n_phys, PAGE_ROWS, HEAD_DIM)
    cv = cache_v.reshape(depth, n_phys, PAGE_ROWS, HEAD_DIM)
    hp, hs = x_prompt[0], x_sample[:, 0]
    outs = [[] for _ in range(8)]
    for l in range(depth):
        last = l == depth - 1
        params = (norm_g[l], w_in[l], conv_w[l], conv_b[l], dt_bias[l], a_log[l], d_skip[l], ssd_norm_g[l],
                  w_out[l], final_norm_g, last)
        hp, kp, vp, cp, sp = _prompt_layer(hp, *params, PROMPT_ROW_TILE)
        hs, ks, vs, cs, ss = _sample_layer(hs, l, ck, cv, state_conv, state_ssm, page_table, *params)
        kv4 = lambda a, n: a.reshape(n, -1, N_KV_HEADS, HEAD_DIM)
        for lst, val in zip(outs, (kv4(kp, batch), kv4(vp, batch), cp[None], sp[None],
                                   kv4(ks, n_seq), kv4(vs, n_seq), cs, ss)):
            lst.append(val)
    return (hp[None], hs[:, None]) + tuple(jnp.stack(o) for o in outs)
```

```python
import functools

import jax
import jax.numpy as jnp
from jax import lax
from jax.experimental import pallas as pl
from jax.experimental.pallas import tpu as pltpu

F32 = jnp.float32
BF16 = jnp.bfloat16

D_MODEL = 1024
N_ATT_HEADS = 8
N_KV_HEADS = 2
HEAD_DIM = 128
Q_PER_KV = N_ATT_HEADS // N_KV_HEADS
ATT_WIDTH = N_ATT_HEADS * HEAD_DIM
KV_WIDTH = N_KV_HEADS * HEAD_DIM
MOBA_BLOCK = 256
MOBA_TOPK = 3
PAGE_SIZE = 128
SSD_WIDTH = 1024
SSD_HEAD_DIM = 64
N_SSD_HEADS = SSD_WIDTH // SSD_HEAD_DIM
SSD_GROUPS = 2
HEADS_PER_GROUP = N_SSD_HEADS // SSD_GROUPS
GROUP_WIDTH = HEADS_PER_GROUP * SSD_HEAD_DIM
D_STATE = 128
CONV_WIDTH = 4
SSD_CHUNK = 256
CONV_DIM = SSD_WIDTH + 2 * SSD_GROUPS * D_STATE
MIX_WIDTH = ATT_WIDTH + SSD_WIDTH
Q_END = ATT_WIDTH
K_END = Q_END + KV_WIDTH
V_END = K_END + KV_WIDTH
G_END = V_END + ATT_WIDTH
Z_END = G_END + SSD_WIDTH
XBC_END = Z_END + CONV_DIM
IN_WIDTH = XBC_END + N_SSD_HEADS
NORM_EPS = 1e-5
ATT_SCALE = HEAD_DIM ** -0.5
EXP2_SCALE = ATT_SCALE * 1.4426950408889634

LANES = 128
SUBLANES = 8
VMEM_LIMIT_BYTES = 56 * 1024 * 1024

MASK_BIAS = -1e30

NT_DIMS = (((1,), (1,)), ((), ()))
TN_DIMS = (((0,), (0,)), ((), ()))


def _cparams(*semantics):
    return pltpu.CompilerParams(dimension_semantics=semantics, vmem_limit_bytes=VMEM_LIMIT_BYTES)


def _resident(shape):
    nd = len(shape)
    return pl.BlockSpec(shape, lambda *_: (0,) * nd, pipeline_mode=pl.Buffered(1))


def _rmsnorm(x, g):
    return x * lax.rsqrt(jnp.mean(x * x, axis=-1, keepdims=True) + NORM_EPS) * g


def _silu(x):
    return x * jax.nn.sigmoid(x)


def _reduce_rows(x, op, final):
    slabs = [x[r:r + SUBLANES] for r in range(0, x.shape[0], SUBLANES)]
    while len(slabs) > 1:
        nxt = [op(slabs[a], slabs[a + 1]) for a in range(0, len(slabs) - 1, 2)]
        if len(slabs) % 2:
            nxt.append(slabs[-1])
        slabs = nxt
    return final(slabs[0], axis=0, keepdims=True)


def _split3(x):
    hi = x.astype(BF16)
    r = x - hi.astype(F32)
    mid = r.astype(BF16)
    lo = (r - mid.astype(F32)).astype(BF16)
    return hi, mid, lo


def _dot3(parts, m, dims=None):
    if dims is None:
        f = lambda p: jnp.dot(p, m, preferred_element_type=F32)
    else:
        f = lambda p: lax.dot_general(p, m, dims, preferred_element_type=F32)
    return (f(parts[2]) + f(parts[1])) + f(parts[0])


def _in_proj_kernel(x_ref, g_ref, wqT_ref, wkv_ref, wvT_ref, wrest_ref, wdt_ref,
                    qT_ref, k_ref, v_ref, ka_ref, vT_ref, gz_ref, xbc_ref, dt_ref, ksum_ref):
    tm = x_ref.shape[0]
    xn = _rmsnorm(x_ref[...], g_ref[...]).astype(BF16)
    qT_ref[...] = lax.dot_general(wqT_ref[...], xn, NT_DIMS, preferred_element_type=F32).astype(BF16)
    vT = lax.dot_general(wvT_ref[...], xn, NT_DIMS, preferred_element_type=F32).astype(BF16)
    kv = jnp.dot(xn, wkv_ref[...], preferred_element_type=F32)
    k = kv[:, :KV_WIDTH]
    k_ref[...] = k
    v_ref[...] = kv[:, KV_WIDTH:]
    blk0 = pl.program_id(0) * (tm // MOBA_BLOCK)
    lane = lax.broadcasted_iota(jnp.int32, (MOBA_BLOCK, LANES), 1)
    for b in range(tm // MOBA_BLOCK):
        rows = slice(b * MOBA_BLOCK, (b + 1) * MOBA_BLOCK)
        onehot = (lane == blk0 + b).astype(BF16)
        for h in range(N_KV_HEADS):
            ka_ref[h, b, :, :HEAD_DIM] = k[rows, h * HEAD_DIM:(h + 1) * HEAD_DIM].astype(BF16)
            ka_ref[h, b, :, HEAD_DIM:] = onehot
        vT_ref[b] = vT[:, rows]
        ksum_ref[0, b:b + 1, :] = jnp.sum(k[rows], axis=0, keepdims=True)
    rest = jnp.dot(xn, wrest_ref[...], preferred_element_type=F32)
    gz_ref[...] = rest[:, :ATT_WIDTH + SSD_WIDTH]
    xbc_ref[...] = rest[:, ATT_WIDTH + SSD_WIDTH:]
    dt_ref[...] = jnp.dot(xn, wdt_ref[...], preferred_element_type=F32)


def _in_proj_weights(w_in):
    wb = w_in.astype(BF16)
    wqT = wb[:, :Q_END].T
    wkv = wb[:, Q_END:V_END]
    wvT = wb[:, K_END:V_END].T
    wrest = wb[:, V_END:XBC_END]
    wdt = jnp.pad(wb[:, XBC_END:], ((0, 0), (0, LANES - N_SSD_HEADS)))
    return wqT, wkv, wvT, wrest, wdt


def _in_proj(x, norm_g, weights, tm):
    n = x.shape[0]
    assert n % tm == 0 and tm % MOBA_BLOCK == 0
    wqT, wkv, wvT, wrest, wdt = weights
    nb = tm // MOBA_BLOCK
    row = lambda i: (i, 0)
    out_shape = (
        jax.ShapeDtypeStruct((ATT_WIDTH, n), BF16),
        jax.ShapeDtypeStruct((n, KV_WIDTH), F32),
        jax.ShapeDtypeStruct((n, KV_WIDTH), F32),
        jax.ShapeDtypeStruct((N_KV_HEADS, n // MOBA_BLOCK, MOBA_BLOCK, 2 * HEAD_DIM), BF16),
        jax.ShapeDtypeStruct((n // MOBA_BLOCK, KV_WIDTH, MOBA_BLOCK), BF16),
        jax.ShapeDtypeStruct((n, ATT_WIDTH + SSD_WIDTH), F32),
        jax.ShapeDtypeStruct((n, CONV_DIM), F32),
        jax.ShapeDtypeStruct((n, LANES), F32),
        jax.ShapeDtypeStruct((n // tm, nb, KV_WIDTH), F32),
    )
    out_specs = (
        pl.BlockSpec((ATT_WIDTH, tm), lambda i: (0, i)),
        pl.BlockSpec((tm, KV_WIDTH), row),
        pl.BlockSpec((tm, KV_WIDTH), row),
        pl.BlockSpec((N_KV_HEADS, nb, MOBA_BLOCK, 2 * HEAD_DIM), lambda i: (0, i, 0, 0)),
        pl.BlockSpec((nb, KV_WIDTH, MOBA_BLOCK), lambda i: (i, 0, 0)),
        pl.BlockSpec((tm, ATT_WIDTH + SSD_WIDTH), row),
        pl.BlockSpec((tm, CONV_DIM), row),
        pl.BlockSpec((tm, LANES), row),
        pl.BlockSpec((1, nb, KV_WIDTH), lambda i: (i, 0, 0)),
    )
    in_specs = [pl.BlockSpec((tm, D_MODEL), row), _resident((1, D_MODEL)), _resident(wqT.shape),
                _resident(wkv.shape), _resident(wvT.shape), _resident(wrest.shape), _resident(wdt.shape)]
    return pl.pallas_call(
        _in_proj_kernel, out_shape=out_shape, grid=(n // tm,), in_specs=in_specs, out_specs=out_specs,
        compiler_params=_cparams("parallel"), name="in_proj",
    )(x, norm_g.reshape(1, D_MODEL), wqT, wkv, wvT, wrest, wdt)


def _moba_prompt_kernel(qT_ref, ka_ref, vT_ref, ksum_ref, g_ref, o_ref, qa_ref, m_ref, l_ref, acc_ref, s0_ref,
                        s1_ref):
    i = pl.program_id(1)
    n_blk = ksum_ref.shape[0]
    kbar = (ksum_ref[...] * (1.0 / MOBA_BLOCK)).astype(BF16)
    blk = lax.broadcasted_iota(jnp.int32, (n_blk, MOBA_BLOCK), 0)
    key_pos = lax.broadcasted_iota(jnp.int32, (MOBA_BLOCK, MOBA_BLOCK), 0)
    qry_pos = lax.broadcasted_iota(jnp.int32, (MOBA_BLOCK, MOBA_BLOCK), 1)

    for h in range(Q_PER_KV):
        cols = slice(h * MOBA_BLOCK, (h + 1) * MOBA_BLOCK)
        qT = qT_ref[h * HEAD_DIM:(h + 1) * HEAD_DIM, :]
        gate = jnp.dot(kbar, qT, preferred_element_type=F32)
        gate = jnp.where(blk < i, gate, -jnp.inf)
        sel = blk < 0
        for r in range(MOBA_TOPK):
            best = jnp.max(gate, axis=0, keepdims=True)
            first = jnp.min(jnp.where(gate == best, blk, n_blk), axis=0, keepdims=True)
            pick = jnp.logical_and(blk == first, r < i)
            sel = jnp.logical_or(sel, pick)
            gate = jnp.where(pick, -jnp.inf, gate)
        qa_ref[:HEAD_DIM, cols] = qT
        qa_ref[HEAD_DIM:HEAD_DIM + n_blk, cols] = jnp.where(sel, 0.0, MASK_BIAS).astype(BF16)
        if n_blk < LANES:
            qa_ref[HEAD_DIM + n_blk:, cols] = jnp.zeros((LANES - n_blk, MOBA_BLOCK), BF16)

    k_own = ka_ref[i][:, :HEAD_DIM]
    for h in range(Q_PER_KV):
        s = jnp.dot(k_own, qT_ref[h * HEAD_DIM:(h + 1) * HEAD_DIM, :], preferred_element_type=F32)
        s = jnp.where(key_pos <= qry_pos, s, MASK_BIAS)
        m_new = _reduce_rows(s, jnp.maximum, jnp.max)
        p = jnp.exp2((s - m_new) * EXP2_SCALE)
        m_ref[h] = m_new
        l_ref[h] = _reduce_rows(p, jnp.add, jnp.sum)
        acc_ref[h] = jnp.dot(vT_ref[i], p.astype(BF16), preferred_element_type=F32)

    n_pairs = (i + 1) // 2
    last_pair = n_blk // 2 - 1

    def logits(pair, dst_ref):
        k_aug = ka_ref[pl.ds(2 * pair, 2)].reshape(2 * MOBA_BLOCK, 2 * HEAD_DIM)
        for h in range(Q_PER_KV):
            dst_ref[h] = jnp.dot(k_aug, qa_ref[:, h * MOBA_BLOCK:(h + 1) * MOBA_BLOCK],
                                 preferred_element_type=F32)

    def softmax_step(pair, cur_ref, nxt_ref):
        logits(jnp.minimum(pair + 1, last_pair), nxt_ref)
        vT = jnp.concatenate([vT_ref[2 * pair], vT_ref[2 * pair + 1]], axis=1)
        for h in range(Q_PER_KV):
            s = cur_ref[h]
            m_old = m_ref[h]
            m_new = jnp.maximum(m_old, _reduce_rows(s, jnp.maximum, jnp.max))
            alpha = jnp.exp2((m_old - m_new) * EXP2_SCALE)
            p = jnp.exp2((s - m_new) * EXP2_SCALE)
            l_ref[h] = alpha * l_ref[h] + _reduce_rows(p, jnp.add, jnp.sum)
            acc_ref[h] = alpha * acc_ref[h] + jnp.dot(vT, p.astype(BF16), preferred_element_type=F32)
            m_ref[h] = m_new

    @pl.when(n_pairs > 0)
    def _():
        logits(0, s0_ref)

    def two_pairs(t, carry):
        softmax_step(2 * t, s0_ref, s1_ref)
        softmax_step(2 * t + 1, s1_ref, s0_ref)
        return carry

    lax.fori_loop(0, (n_pairs + 1) // 2, two_pairs, 0)

    for h in range(Q_PER_KV):
        out = (acc_ref[h] / l_ref[h]).T
        dcols = slice(h * HEAD_DIM, (h + 1) * HEAD_DIM)
        o_ref[:, dcols] = (out * _silu(g_ref[:, dcols])).astype(BF16)


def _moba_prompt(qT, ka, vT, ksum, gz):
    n = qT.shape[1]
    n_blk = n // MOBA_BLOCK
    assert n_blk <= LANES and n_blk % 4 == 0
    kv_rows = Q_PER_KV * HEAD_DIM
    return pl.pallas_call(
        _moba_prompt_kernel,
        out_shape=jax.ShapeDtypeStruct((n, ATT_WIDTH), BF16),
        grid=(N_KV_HEADS, n_blk),
        in_specs=[
            pl.BlockSpec((kv_rows, MOBA_BLOCK), lambda h, i: (h, i)),
            pl.BlockSpec((None, n_blk, MOBA_BLOCK, 2 * HEAD_DIM), lambda h, i: (h, 0, 0, 0)),
            pl.BlockSpec((n_blk, HEAD_DIM, MOBA_BLOCK), lambda h, i: (0, h, 0)),
            pl.BlockSpec((n_blk, HEAD_DIM), lambda h, i: (0, h)),
            pl.BlockSpec((MOBA_BLOCK, kv_rows), lambda h, i: (i, h)),
        ],
        out_specs=pl.BlockSpec((MOBA_BLOCK, kv_rows), lambda h, i: (i, h)),
        scratch_shapes=[pltpu.VMEM((2 * HEAD_DIM, Q_PER_KV * MOBA_BLOCK), BF16),
                        pltpu.VMEM((Q_PER_KV, 1, MOBA_BLOCK), F32),
                        pltpu.VMEM((Q_PER_KV, 1, MOBA_BLOCK), F32),
                        pltpu.VMEM((Q_PER_KV, HEAD_DIM, MOBA_BLOCK), F32),
                        pltpu.VMEM((Q_PER_KV, 2 * MOBA_BLOCK, MOBA_BLOCK), F32),
                        pltpu.VMEM((Q_PER_KV, 2 * MOBA_BLOCK, MOBA_BLOCK), F32)],
        compiler_params=_cparams("parallel", "parallel"), name="moba_prompt",
    )(qT, ka, vT, ksum, gz)


def _ssd_prompt_kernel(xbc_ref, z_ref, dt_ref, convw_ref, convb_ref, dtb_ref, acoef_ref, dskip_ref, ng_ref,
                       o_ref, state_ref, xpad_ref, st_ref, y_ref):
    c = pl.program_id(0)
    t = SSD_CHUNK

    @pl.when(c == 0)
    def _():
        xpad_ref[:SUBLANES, :] = jnp.zeros((SUBLANES, CONV_DIM), F32)
        st_ref[...] = jnp.zeros_like(st_ref)

    xpad_ref[SUBLANES:, :] = xbc_ref[...]
    conv = convb_ref[...] + convw_ref[CONV_WIDTH - 1:CONV_WIDTH, :] * xpad_ref[SUBLANES:, :]
    for w in range(CONV_WIDTH - 1):
        back = CONV_WIDTH - 1 - w
        conv = conv + convw_ref[w:w + 1, :] * xpad_ref[SUBLANES - back:SUBLANES - back + t, :]
    xpad_ref[:SUBLANES, :] = xpad_ref[t:t + SUBLANES, :]
    xc = _silu(conv)
    xs = xc[:, :SSD_WIDTH]

    dt = jax.nn.softplus(dt_ref[...] + dtb_ref[...])
    a = dt * acoef_ref[...]
    row = lax.broadcasted_iota(jnp.int32, (t, t), 0)
    colm = lax.broadcasted_iota(jnp.int32, (t, t), 1)
    lower = row >= colm
    tril = lower.astype(BF16)
    a_hi, a_mid, a_lo = _split3(a)
    a_cum = ((jnp.dot(tril, a_lo, preferred_element_type=F32) + jnp.dot(tril, a_mid, preferred_element_type=F32))
             + jnp.dot(tril, a_hi, preferred_element_type=F32))
    a_cum_t = a_cum.T
    a_end = a_cum[t - 1:t, :]

    for g in range(SSD_GROUPS):
        b_g = xc[:, SSD_WIDTH + g * D_STATE:SSD_WIDTH + (g + 1) * D_STATE]
        c_g = xc[:, SSD_WIDTH + (SSD_GROUPS + g) * D_STATE:SSD_WIDTH + (SSD_GROUPS + g + 1) * D_STATE]
        b_bf = b_g.astype(BF16)
        c_bf = c_g.astype(BF16)
        cb = lax.dot_general(c_bf, b_bf, NT_DIMS, preferred_element_type=F32)
        for hh in range(HEADS_PER_GROUP):
            h = g * HEADS_PER_GROUP + hh
            hc = slice(h * SSD_HEAD_DIM, (h + 1) * SSD_HEAD_DIM)
            acol = a_cum[:, h:h + 1]
            arow = a_cum_t[h:h + 1, :]
            xs_h = xs[:, hc]
            xdt = xs_h * dt[:, h:h + 1]
            decay = jnp.exp(jnp.where(lower, acol - arow, -jnp.inf))
            y = jnp.dot((cb * decay).astype(BF16), xdt.astype(BF16), preferred_element_type=F32)
            st = st_ref[h]
            y = y + lax.dot_general(c_bf, st.astype(BF16), NT_DIMS, preferred_element_type=F32) * jnp.exp(acol)
            xdtd = xdt * jnp.exp(a_end[:, h:h + 1] - acol)
            st_ref[h] = st * jnp.exp(a_end[:, h:h + 1]) + lax.dot_general(
                xdtd.astype(BF16), b_bf, TN_DIMS, preferred_element_type=F32)
            y_ref[:, hc] = y + dskip_ref[:, h:h + 1] * xs_h

    y = y_ref[...] * _silu(z_ref[...])
    o_ref[...] = _rmsnorm(y, ng_ref[...]).astype(o_ref.dtype)

    @pl.when(c == pl.num_programs(0) - 1)
    def _():
        state_ref[...] = st_ref[...]


def _ssd_prompt(xbc, gz, dt_raw, conv_w, conv_b, dt_bias, a_log, d_skip, ssd_norm_g):
    n = xbc.shape[0]
    t = SSD_CHUNK
    assert n % t == 0
    pad = lambda v: jnp.pad(v.reshape(1, N_SSD_HEADS), ((0, 0), (0, LANES - N_SSD_HEADS)))
    a_coef = -jnp.exp(a_log.astype(F32))
    row = lambda c: (c, 0)
    return pl.pallas_call(
        _ssd_prompt_kernel,
        out_shape=(jax.ShapeDtypeStruct((n, SSD_WIDTH), BF16),
                   jax.ShapeDtypeStruct((N_SSD_HEADS, SSD_HEAD_DIM, D_STATE), F32)),
        grid=(n // t,),
        in_specs=[pl.BlockSpec((t, CONV_DIM), row),
                  pl.BlockSpec((t, SSD_WIDTH), lambda c: (c, 1)),
                  pl.BlockSpec((t, LANES), row),
                  _resident((CONV_WIDTH, CONV_DIM)), _resident((1, CONV_DIM)), _resident((1, LANES)),
                  _resident((1, LANES)), _resident((1, LANES)), _resident((1, SSD_WIDTH))],
        out_specs=(pl.BlockSpec((t, SSD_WIDTH), row),
                   pl.BlockSpec((N_SSD_HEADS, SSD_HEAD_DIM, D_STATE), lambda c: (0, 0, 0))),
        scratch_shapes=[pltpu.VMEM((t + SUBLANES, CONV_DIM), F32),
                        pltpu.VMEM((N_SSD_HEADS, SSD_HEAD_DIM, D_STATE), F32),
                        pltpu.VMEM((t, SSD_WIDTH), F32)],
        compiler_params=_cparams("arbitrary"), name="ssd_prompt",
    )(xbc, gz, dt_raw, conv_w, conv_b.reshape(1, CONV_DIM), pad(dt_bias), pad(a_coef), pad(d_skip),
      ssd_norm_g.reshape(1, SSD_WIDTH))


def _out_proj_kernel(x_ref, oa_ref, os_ref, w_ref, fg_ref, h_ref, *, final_norm):
    out = jnp.dot(oa_ref[...], w_ref[:ATT_WIDTH, :], preferred_element_type=F32)
    out = out + jnp.dot(os_ref[...], w_ref[ATT_WIDTH:, :], preferred_element_type=F32)
    h = x_ref[...] + out
    h_ref[...] = _rmsnorm(h, fg_ref[...]) if final_norm else h


def _out_proj(x, o_att, o_ssd, w_out_bf, final_g, final_norm, tm):
    n = x.shape[0]
    assert n % tm == 0
    row = lambda i: (i, 0)
    return pl.pallas_call(
        functools.partial(_out_proj_kernel, final_norm=final_norm),
        out_shape=jax.ShapeDtypeStruct((n, D_MODEL), F32),
        grid=(n // tm,),
        in_specs=[pl.BlockSpec((tm, D_MODEL), row), pl.BlockSpec((tm, ATT_WIDTH), row),
                  pl.BlockSpec((tm, SSD_WIDTH), row), _resident((MIX_WIDTH, D_MODEL)), _resident((1, D_MODEL))],
        out_specs=pl.BlockSpec((tm, D_MODEL), row),
        compiler_params=_cparams("parallel"), name="out_proj",
    )(x, o_att, o_ssd, w_out_bf, final_g.reshape(1, D_MODEL))


def _prompt_layer(x, norm_g, w_in, conv_w, conv_b, dt_bias, a_log, d_skip, ssd_norm_g, w_out, final_g,
                  final_norm, tm):
    n = x.shape[0]
    qT, k, v, ka, vT, gz, xbc, dt_raw, ksum = _in_proj(x, norm_g, _in_proj_weights(w_in), tm)
    o_att = _moba_prompt(qT, ka, vT, ksum.reshape(n // MOBA_BLOCK, KV_WIDTH), gz)
    o_ssd, state = _ssd_prompt(xbc, gz, dt_raw, conv_w, conv_b, dt_bias, a_log, d_skip, ssd_norm_g)
    h = _out_proj(x, o_att, o_ssd, w_out.astype(BF16), final_g, final_norm, tm)
    return h, k, v, xbc[n - (CONV_WIDTH - 1):], state


PAGES_PER_BLOCK = MOBA_BLOCK // PAGE_SIZE
PAGE_ROWS = PAGE_SIZE * N_KV_HEADS
SAMPLE_U_WIDTH = 49 * LANES
KBAR_GROUP = 8


def _in_proj_sample_kernel(x_ref, g_ref, w_ref, u_ref):
    xn = _rmsnorm(x_ref[...], g_ref[...]).astype(BF16)
    u_ref[...] = jnp.dot(xn, w_ref[...], preferred_element_type=F32)


def _in_proj_sample(x, norm_g, w_in):
    n = x.shape[0]
    wb = w_in.astype(BF16)
    w = jnp.concatenate([wb, jnp.repeat(wb[:, XBC_END:], SSD_HEAD_DIM, axis=1)], axis=1)
    w = jnp.pad(w, ((0, 0), (0, SAMPLE_U_WIDTH - w.shape[1])))
    return pl.pallas_call(
        _in_proj_sample_kernel, out_shape=jax.ShapeDtypeStruct((n, SAMPLE_U_WIDTH), F32), grid=(1,),
        in_specs=[_resident((n, D_MODEL)), _resident((1, D_MODEL)), _resident(w.shape)],
        out_specs=pl.BlockSpec((n, SAMPLE_U_WIDTH), lambda i: (0, 0)),
        compiler_params=_cparams("arbitrary"), name="in_proj_sample",
    )(x, norm_g.reshape(1, D_MODEL), w)


def _kbar_sample_kernel(pt_ref, cache_ref, out_ref, buf_ref, sem_ref, *, layer):
    b = pl.program_id(0)
    n_groups = pt_ref.shape[1] // KBAR_GROUP

    def copies(g, slot):
        return [pltpu.make_async_copy(cache_ref.at[layer, pt_ref[b, g * KBAR_GROUP + p]],
                                      buf_ref.at[slot, p], sem_ref.at[slot]) for p in range(KBAR_GROUP)]

    for cp in copies(0, 0):
        cp.start()
    for g in range(n_groups):
        slot = g % 2
        if g + 1 < n_groups:
            for cp in copies(g + 1, 1 - slot):
                cp.start()
        for cp in copies(g, slot):
            cp.wait()
        for bb in range(KBAR_GROUP // PAGES_PER_BLOCK):
            acc = jnp.zeros((SUBLANES, HEAD_DIM), F32)
            for p in range(PAGES_PER_BLOCK):
                page = buf_ref[slot, bb * PAGES_PER_BLOCK + p]
                acc = acc + jnp.sum(page.reshape(PAGE_ROWS // SUBLANES, SUBLANES, HEAD_DIM), axis=0)
            acc = acc + pltpu.roll(acc, 2, 0)
            acc = acc + pltpu.roll(acc, 4, 0)
            blk = g * (KBAR_GROUP // PAGES_PER_BLOCK) + bb
            for h in range(N_KV_HEADS):
                out_ref[0, h, blk:blk + 1, :] = acc[h:h + 1, :] * (1.0 / MOBA_BLOCK)


def _kbar_sample(cache_pages, page_table, layer):
    n_seq, n_pages = page_table.shape
    assert n_pages % KBAR_GROUP == 0
    n_blk = n_pages // PAGES_PER_BLOCK
    return pl.pallas_call(
        functools.partial(_kbar_sample_kernel, layer=layer),
        out_shape=jax.ShapeDtypeStruct((n_seq, N_KV_HEADS, n_blk, HEAD_DIM), F32),
        grid_spec=pltpu.PrefetchScalarGridSpec(
            num_scalar_prefetch=1, grid=(n_seq,),
            in_specs=[pl.BlockSpec(memory_space=pl.ANY)],
            out_specs=pl.BlockSpec((1, N_KV_HEADS, n_blk, HEAD_DIM), lambda b, pt: (b, 0, 0, 0)),
            scratch_shapes=[pltpu.VMEM((2, KBAR_GROUP, PAGE_ROWS, HEAD_DIM), F32),
                            pltpu.SemaphoreType.DMA((2,))]),
        compiler_params=_cparams("arbitrary"), name="kbar_sample",
    )(page_table, cache_pages)


def _gate_sample_kernel(q_ref, kbar_ref, idx_ref, gate_ref):
    n_seq, _, n_blk, _ = kbar_ref.shape
    for b in range(n_seq):
        for kh in range(N_KV_HEADS):
            r0 = b * N_ATT_HEADS + kh * Q_PER_KV
            q4 = q_ref[r0:r0 + Q_PER_KV, :].astype(BF16)
            gate_ref[r0:r0 + Q_PER_KV, :] = lax.dot_general(
                q4, kbar_ref[b, kh].astype(BF16), NT_DIMS, preferred_element_type=F32)
    gate = gate_ref[...]
    lane = lax.broadcasted_iota(jnp.int32, gate.shape, 1)
    idx_ref[...] = jnp.zeros_like(idx_ref)
    for r in range(MOBA_TOPK):
        best = jnp.max(gate, axis=1, keepdims=True)
        first = jnp.min(jnp.where(gate == best, lane, n_blk), axis=1, keepdims=True)
        idx_ref[:, r:r + 1] = first
        gate = jnp.where(lane == first, -jnp.inf, gate)


def _gate_sample(q_rows, kbar):
    n_seq, _, n_blk, _ = kbar.shape
    assert n_blk >= MOBA_TOPK
    rows = n_seq * N_ATT_HEADS
    return pl.pallas_call(
        _gate_sample_kernel, out_shape=jax.ShapeDtypeStruct((rows, LANES), jnp.int32), grid=(1,),
        in_specs=[_resident((rows, HEAD_DIM)), _resident(kbar.shape)],
        out_specs=pl.BlockSpec((rows, LANES), lambda i: (0, 0)),
        scratch_shapes=[pltpu.VMEM((rows, n_blk), F32)],
        compiler_params=_cparams("arbitrary"), name="gate_sample",
    )(q_rows, kbar)


def _attn_sample_kernel(pt_ref, idx_ref, q_ref, knew_ref, vnew_ref, g_ref, ck_ref, cv_ref, o_ref,
                        kbuf_ref, vbuf_ref, sem_ref, *, layer):
    b = pl.program_id(0)
    n_sel = N_ATT_HEADS * MOBA_TOPK

    def copies():
        out = []
        for s in range(n_sel):
            blk = idx_ref[b * n_sel + s]
            for p in range(PAGES_PER_BLOCK):
                page = pt_ref[b, blk * PAGES_PER_BLOCK + p]
                slot = s * PAGES_PER_BLOCK + p
                out.append(pltpu.make_async_copy(ck_ref.at[layer, page], kbuf_ref.at[slot], sem_ref.at[0]))
                out.append(pltpu.make_async_copy(cv_ref.at[layer, page], vbuf_ref.at[slot], sem_ref.at[1]))
        return out

    cps = copies()
    for cp in cps:
        cp.start()
    for cp in cps:
        cp.wait()

    row_head = lax.broadcasted_iota(jnp.int32, (1, PAGE_ROWS), 1) % N_KV_HEADS
    for h in range(N_ATT_HEADS):
        kh = h // Q_PER_KV
        qh = q_ref[0, h:h + 1, :].astype(BF16)
        q8 = jnp.broadcast_to(qh, (SUBLANES, HEAD_DIM))
        valid = row_head == kh
        scores = []
        for s in range(MOBA_TOPK * PAGES_PER_BLOCK):
            page = kbuf_ref[h * MOBA_TOPK * PAGES_PER_BLOCK + s].astype(BF16)
            sc = lax.dot_general(q8, page, NT_DIMS, preferred_element_type=F32)[0:1, :]
            scores.append(jnp.where(valid, sc, -jnp.inf))
        k_self = knew_ref[0, kh:kh + 1, :].astype(BF16).astype(F32)
        s_self = jnp.sum(qh.astype(F32) * k_self, axis=1, keepdims=True)
        m = s_self
        for sc in scores:
            m = jnp.maximum(m, jnp.max(sc, axis=1, keepdims=True))
        p_self = jnp.exp((s_self - m) * ATT_SCALE)
        l = p_self
        acc = p_self.astype(BF16).astype(F32) * vnew_ref[0, kh:kh + 1, :].astype(BF16).astype(F32)
        for s, sc in enumerate(scores):
            p = jnp.exp((sc - m) * ATT_SCALE)
            l = l + jnp.sum(p, axis=1, keepdims=True)
            p8 = jnp.broadcast_to(p.astype(BF16), (SUBLANES, PAGE_ROWS))
            page = vbuf_ref[h * MOBA_TOPK * PAGES_PER_BLOCK + s].astype(BF16)
            acc = acc + jnp.dot(p8, page, preferred_element_type=F32)[0:1, :]
        o_ref[0, h:h + 1, :] = ((acc / l) * _silu(g_ref[0, h:h + 1, :])).astype(o_ref.dtype)


def _attn_sample(q3, knew3, vnew3, g3, cache_k_pages, cache_v_pages, page_table, idx_flat, layer):
    n_seq = q3.shape[0]
    n_slots = N_ATT_HEADS * MOBA_TOPK * PAGES_PER_BLOCK
    per_seq = lambda r: pl.BlockSpec((1, r, HEAD_DIM), lambda b, pt, ix: (b, 0, 0))
    return pl.pallas_call(
        functools.partial(_attn_sample_kernel, layer=layer),
        out_shape=jax.ShapeDtypeStruct((n_seq, N_ATT_HEADS, HEAD_DIM), BF16),
        grid_spec=pltpu.PrefetchScalarGridSpec(
            num_scalar_prefetch=2, grid=(n_seq,),
            in_specs=[per_seq(N_ATT_HEADS), per_seq(N_KV_HEADS), per_seq(N_KV_HEADS), per_seq(N_ATT_HEADS),
                      pl.BlockSpec(memory_space=pl.ANY), pl.BlockSpec(memory_space=pl.ANY)],
            out_specs=per_seq(N_ATT_HEADS),
            scratch_shapes=[pltpu.VMEM((n_slots, PAGE_ROWS, HEAD_DIM), F32),
                            pltpu.VMEM((n_slots, PAGE_ROWS, HEAD_DIM), F32),
                            pltpu.SemaphoreType.DMA((2,))]),
        compiler_params=_cparams("arbitrary"), name="attn_sample",
    )(page_table, idx_flat, q3, knew3, vnew3, g3, cache_k_pages, cache_v_pages)


def _column(row):
    return jnp.broadcast_to(row, (LANES, LANES)).T


def _ssd_sample_kernel(xbc_ref, z_ref, dt_ref, sc_ref, ss_ref, convw_ref, convb_ref, dtb_ref, acoef_ref,
                       dskip_ref, ng_ref, o_ref, nc_ref, ns_ref, y_ref):
    hist = sc_ref[0]
    new = xbc_ref[0]
    conv = convb_ref[...] + convw_ref[CONV_WIDTH - 1:CONV_WIDTH, :] * new
    for w in range(CONV_WIDTH - 1):
        conv = conv + convw_ref[w:w + 1, :] * hist[w:w + 1, :]
    nc_ref[0, :CONV_WIDTH - 2, :] = hist[1:, :]
    nc_ref[0, CONV_WIDTH - 2:, :] = new
    xc = _silu(conv)
    xs = xc[:, :SSD_WIDTH]
    dt = jax.nn.softplus(dt_ref[0] + dtb_ref[...])
    d_a = jnp.exp(dt * acoef_ref[...])
    xdt = xs * dt
    heads_per_chunk = LANES // SSD_HEAD_DIM
    for c in range(SSD_WIDTH // LANES):
        lanes = slice(c * LANES, (c + 1) * LANES)
        x_col = _column(xdt[:, lanes])
        da_col = _column(d_a[:, lanes])
        y_cols = []
        for hh in range(heads_per_chunk):
            h = c * heads_per_chunk + hh
            g = h // HEADS_PER_GROUP
            rows = slice(hh * SSD_HEAD_DIM, (hh + 1) * SSD_HEAD_DIM)
            b_g = xc[:, SSD_WIDTH + g * D_STATE:SSD_WIDTH + (g + 1) * D_STATE]
            c_g = xc[:, SSD_WIDTH + (SSD_GROUPS + g) * D_STATE:SSD_WIDTH + (SSD_GROUPS + g + 1) * D_STATE]
            st = ss_ref[0, h] * da_col[rows, :] + x_col[rows, :] * b_g
            ns_ref[0, h] = st
            y_cols.append(jnp.sum(st * c_g, axis=1, keepdims=True))
        y_col = jnp.concatenate(y_cols, axis=0)
        y_ref[:, lanes] = jnp.broadcast_to(y_col, (LANES, LANES)).T[0:1, :]
    y = (y_ref[...] + dskip_ref[...] * xs) * _silu(z_ref[0])
    o_ref[0] = _rmsnorm(y, ng_ref[...]).astype(o_ref.dtype)


def _ssd_sample(xbc3, z3, dte3, state_conv, state_ssm, layer, conv_w, conv_b, dt_bias, a_log, d_skip, ssd_norm_g):
    n_seq = xbc3.shape[0]
    per_channel = lambda v: jnp.repeat(v.astype(F32), SSD_HEAD_DIM).reshape(1, SSD_WIDTH)
    seq3 = lambda w: pl.BlockSpec((1, 1, w), lambda b: (b, 0, 0))
    return pl.pallas_call(
        _ssd_sample_kernel,
        out_shape=(jax.ShapeDtypeStruct((n_seq, 1, SSD_WIDTH), BF16),
                   jax.ShapeDtypeStruct((n_seq, CONV_WIDTH - 1, CONV_DIM), F32),
                   jax.ShapeDtypeStruct((n_seq, N_SSD_HEADS, SSD_HEAD_DIM, D_STATE), F32)),
        grid=(n_seq,),
        in_specs=[seq3(CONV_DIM), seq3(SSD_WIDTH), seq3(SSD_WIDTH),
                  pl.BlockSpec((None, 1, CONV_WIDTH - 1, CONV_DIM), lambda b: (layer, b, 0, 0)),
                  pl.BlockSpec((None, 1, N_SSD_HEADS, SSD_HEAD_DIM, D_STATE), lambda b: (layer, b, 0, 0, 0)),
                  _resident((CONV_WIDTH, CONV_DIM)), _resident((1, CONV_DIM)), _resident((1, SSD_WIDTH)),
                  _resident((1, SSD_WIDTH)), _resident((1, SSD_WIDTH)), _resident((1, SSD_WIDTH))],
        out_specs=(seq3(SSD_WIDTH),
                   pl.BlockSpec((1, CONV_WIDTH - 1, CONV_DIM), lambda b: (b, 0, 0)),
                   pl.BlockSpec((1, N_SSD_HEADS, SSD_HEAD_DIM, D_STATE), lambda b: (b, 0, 0, 0))),
        scratch_shapes=[pltpu.VMEM((1, SSD_WIDTH), F32)],
        compiler_params=_cparams("parallel"), name="ssd_sample",
    )(xbc3, z3, dte3, state_conv, state_ssm, conv_w, conv_b.reshape(1, CONV_DIM), per_channel(dt_bias),
      per_channel(-jnp.exp(a_log.astype(F32))), per_channel(d_skip), ssd_norm_g.reshape(1, SSD_WIDTH))


def _sample_layer(x, layer, cache_k_pages, cache_v_pages, state_conv, state_ssm, page_table,
                  norm_g, w_in, conv_w, conv_b, dt_bias, a_log, d_skip, ssd_norm_g, w_out, final_g, final_norm):
    n_seq = x.shape[0]
    u = _in_proj_sample(x, norm_g, w_in)
    q, k, v, g_att = u[:, :Q_END], u[:, Q_END:K_END], u[:, K_END:V_END], u[:, V_END:G_END]
    z, xbc, dte = u[:, G_END:Z_END], u[:, Z_END:XBC_END], u[:, IN_WIDTH:IN_WIDTH + SSD_WIDTH]
    kbar = _kbar_sample(cache_k_pages, page_table, layer)
    idx = _gate_sample(q.reshape(n_seq * N_ATT_HEADS, HEAD_DIM), kbar)
    heads = lambda a, r: a.reshape(n_seq, r, HEAD_DIM)
    o_att = _attn_sample(heads(q, N_ATT_HEADS), heads(k, N_KV_HEADS), heads(v, N_KV_HEADS),
                         heads(g_att, N_ATT_HEADS), cache_k_pages, cache_v_pages, page_table,
                         idx[:, :MOBA_TOPK].reshape(-1), layer)
    rows3 = lambda a: a.reshape(n_seq, 1, a.shape[1])
    o_ssd, new_conv, new_ssm = _ssd_sample(rows3(xbc), rows3(z), rows3(dte), state_conv, state_ssm, layer,
                                           conv_w, conv_b, dt_bias, a_log, d_skip, ssd_norm_g)
    h = _out_proj(x, o_att.reshape(n_seq, ATT_WIDTH), o_ssd.reshape(n_seq, SSD_WIDTH), w_out.astype(BF16),
                  final_g, final_norm, n_seq)
    return h, k, v, new_conv, new_ssm


PROMPT_ROW_TILE = 512


def kernel(x_prompt, x_sample, cache_k, cache_v, state_conv, state_ssm, page_table, norm_g, w_in, conv_w,
           conv_b, dt_bias, a_log, d_skip, ssd_norm_g, w_out, final_norm_g):
    depth = w_in.shape[0]
    batch, seq, _ = x_prompt.shape
    n_seq, dec_seq, _ = x_sample.shape
    assert batch == 1 and dec_seq == 1
    n_phys = cache_k.shape[1]
    ck = cache_k.reshape(depth, n_phys, PAGE_ROWS, HEAD_DIM)
    cv = cache_v.reshape(depth, n_phys, PAGE_ROWS, HEAD_DIM)
    hp, hs = x_prompt[0], x_sample[:, 0]
    outs = [[] for _ in range(8)]
    for l in range(depth):
        last = l == depth - 1
        params = (norm_g[l], w_in[l], conv_w[l], conv_b[l], dt_bias[l], a_log[l], d_skip[l], ssd_norm_g[l],
                  w_out[l], final_norm_g, last)
        hp, kp, vp, cp, sp = _prompt_layer(hp, *params, PROMPT_ROW_TILE)
        hs, ks, vs, cs, ss = _sample_layer(hs, l, ck, cv, state_conv, state_ssm, page_table, *params)
        kv4 = lambda a, n: a.reshape(n, -1, N_KV_HEADS, HEAD_DIM)
        for lst, val in zip(outs, (kv4(kp, batch), kv4(vp, batch), cp[None], sp[None],
                                   kv4(ks, n_seq), kv4(vs, n_seq), cs, ss)):
            lst.append(val)
    return (hp[None], hs[:, None]) + tuple(jnp.stack(o) for o in outs)
```

```python
import functools

import jax
import jax.numpy as jnp
from jax import lax
from jax.experimental import pallas as pl
from jax.experimental.pallas import tpu as pltpu

F32 = jnp.float32
BF16 = jnp.bfloat16

D_MODEL = 1024
N_ATT_HEADS = 8
N_KV_HEADS = 2
HEAD_DIM = 128
Q_PER_KV = N_ATT_HEADS // N_KV_HEADS
ATT_WIDTH = N_ATT_HEADS * HEAD_DIM
KV_WIDTH = N_KV_HEADS * HEAD_DIM
MOBA_BLOCK = 256
MOBA_TOPK = 3
PAGE_SIZE = 128
SSD_WIDTH = 1024
SSD_HEAD_DIM = 64
N_SSD_HEADS = SSD_WIDTH // SSD_HEAD_DIM
SSD_GROUPS = 2
HEADS_PER_GROUP = N_SSD_HEADS // SSD_GROUPS
GROUP_WIDTH = HEADS_PER_GROUP * SSD_HEAD_DIM
D_STATE = 128
CONV_WIDTH = 4
SSD_CHUNK = 256
CONV_DIM = SSD_WIDTH + 2 * SSD_GROUPS * D_STATE
MIX_WIDTH = ATT_WIDTH + SSD_WIDTH
Q_END = ATT_WIDTH
K_END = Q_END + KV_WIDTH
V_END = K_END + KV_WIDTH
G_END = V_END + ATT_WIDTH
Z_END = G_END + SSD_WIDTH
XBC_END = Z_END + CONV_DIM
IN_WIDTH = XBC_END + N_SSD_HEADS
NORM_EPS = 1e-5
ATT_SCALE = HEAD_DIM ** -0.5
EXP2_SCALE = ATT_SCALE * 1.4426950408889634

LANES = 128
SUBLANES = 8
VMEM_LIMIT_BYTES = 56 * 1024 * 1024

MASK_BIAS = -1e30
VT_ONES = 16
VT_ROWS = HEAD_DIM + VT_ONES
KEY_BLOCKS_PER_STEP = 2

NT_DIMS = (((1,), (1,)), ((), ()))
TN_DIMS = (((0,), (0,)), ((), ()))


def _cparams(*semantics):
    return pltpu.CompilerParams(dimension_semantics=semantics, vmem_limit_bytes=VMEM_LIMIT_BYTES)


def _resident(shape):
    nd = len(shape)
    return pl.BlockSpec(shape, lambda *_: (0,) * nd, pipeline_mode=pl.Buffered(1))


def _rmsnorm(x, g):
    return x * lax.rsqrt(jnp.mean(x * x, axis=-1, keepdims=True) + NORM_EPS) * g


def _silu(x):
    return x * jax.nn.sigmoid(x)


def _reduce_rows(x, op, final):
    slabs = [x[r:r + SUBLANES] for r in range(0, x.shape[0], SUBLANES)]
    while len(slabs) > 1:
        nxt = [op(slabs[a], slabs[a + 1]) for a in range(0, len(slabs) - 1, 2)]
        if len(slabs) % 2:
            nxt.append(slabs[-1])
        slabs = nxt
    return final(slabs[0], axis=0, keepdims=True)


def _split3(x):
    hi = x.astype(BF16)
    r = x - hi.astype(F32)
    mid = r.astype(BF16)
    lo = (r - mid.astype(F32)).astype(BF16)
    return hi, mid, lo


def _dot3(parts, m, dims=None):
    if dims is None:
        f = lambda p: jnp.dot(p, m, preferred_element_type=F32)
    else:
        f = lambda p: lax.dot_general(p, m, dims, preferred_element_type=F32)
    return (f(parts[2]) + f(parts[1])) + f(parts[0])


def _in_proj_kernel(x_ref, g_ref, wqT_ref, wkv_ref, wvT_ref, wrest_ref, wdt_ref,
                    qT_ref, k_ref, v_ref, ka_ref, vT_ref, gz_ref, xbc_ref, dt_ref, ksum_ref):
    tm = x_ref.shape[0]
    xn = _rmsnorm(x_ref[...], g_ref[...]).astype(BF16)
    qT = lax.dot_general(wqT_ref[...], xn, NT_DIMS, preferred_element_type=F32)
    qT_ref[...] = (qT * EXP2_SCALE).astype(BF16)
    vT = lax.dot_general(wvT_ref[...], xn, NT_DIMS, preferred_element_type=F32).astype(BF16)
    kv = jnp.dot(xn, wkv_ref[...], preferred_element_type=F32)
    k = kv[:, :KV_WIDTH]
    k_ref[...] = k
    v_ref[...] = kv[:, KV_WIDTH:]
    blk0 = pl.program_id(0) * (tm // MOBA_BLOCK)
    lane = lax.broadcasted_iota(jnp.int32, (MOBA_BLOCK, LANES), 1)
    for b in range(tm // MOBA_BLOCK):
        rows = slice(b * MOBA_BLOCK, (b + 1) * MOBA_BLOCK)
        onehot = (lane == blk0 + b).astype(BF16)
        for h in range(N_KV_HEADS):
            ka_ref[h, b, :, :HEAD_DIM] = k[rows, h * HEAD_DIM:(h + 1) * HEAD_DIM].astype(BF16)
            ka_ref[h, b, :, HEAD_DIM:] = onehot
        for h in range(N_KV_HEADS):
            vT_ref[b, h * VT_ROWS:h * VT_ROWS + HEAD_DIM, :] = vT[h * HEAD_DIM:(h + 1) * HEAD_DIM, rows]
            vT_ref[b, h * VT_ROWS + HEAD_DIM:(h + 1) * VT_ROWS, :] = jnp.ones((VT_ONES, MOBA_BLOCK), BF16)
        ksum_ref[0, b:b + 1, :] = jnp.sum(k[rows], axis=0, keepdims=True)
    rest = jnp.dot(xn, wrest_ref[...], preferred_element_type=F32)
    gz_ref[...] = rest[:, :ATT_WIDTH + SSD_WIDTH]
    xbc_ref[...] = rest[:, ATT_WIDTH + SSD_WIDTH:]
    dt_ref[...] = jnp.dot(xn, wdt_ref[...], preferred_element_type=F32)


def _in_proj_weights(w_in):
    wb = w_in.astype(BF16)
    wqT = wb[:, :Q_END].T
    wkv = wb[:, Q_END:V_END]
    wvT = wb[:, K_END:V_END].T
    wrest = wb[:, V_END:XBC_END]
    wdt = jnp.pad(wb[:, XBC_END:], ((0, 0), (0, LANES - N_SSD_HEADS)))
    return wqT, wkv, wvT, wrest, wdt


def _in_proj(x, norm_g, weights, tm):
    n = x.shape[0]
    assert n % tm == 0 and tm % MOBA_BLOCK == 0
    wqT, wkv, wvT, wrest, wdt = weights
    nb = tm // MOBA_BLOCK
    row = lambda i: (i, 0)
    out_shape = (
        jax.ShapeDtypeStruct((ATT_WIDTH, n), BF16),
        jax.ShapeDtypeStruct((n, KV_WIDTH), F32),
        jax.ShapeDtypeStruct((n, KV_WIDTH), F32),
        jax.ShapeDtypeStruct((N_KV_HEADS, n // MOBA_BLOCK, MOBA_BLOCK, 2 * HEAD_DIM), BF16),
        jax.ShapeDtypeStruct((n // MOBA_BLOCK, N_KV_HEADS * VT_ROWS, MOBA_BLOCK), BF16),
        jax.ShapeDtypeStruct((n, ATT_WIDTH + SSD_WIDTH), F32),
        jax.ShapeDtypeStruct((n, CONV_DIM), F32),
        jax.ShapeDtypeStruct((n, LANES), F32),
        jax.ShapeDtypeStruct((n // tm, nb, KV_WIDTH), F32),
    )
    out_specs = (
        pl.BlockSpec((ATT_WIDTH, tm), lambda i: (0, i)),
        pl.BlockSpec((tm, KV_WIDTH), row),
        pl.BlockSpec((tm, KV_WIDTH), row),
        pl.BlockSpec((N_KV_HEADS, nb, MOBA_BLOCK, 2 * HEAD_DIM), lambda i: (0, i, 0, 0)),
        pl.BlockSpec((nb, N_KV_HEADS * VT_ROWS, MOBA_BLOCK), lambda i: (i, 0, 0)),
        pl.BlockSpec((tm, ATT_WIDTH + SSD_WIDTH), row),
        pl.BlockSpec((tm, CONV_DIM), row),
        pl.BlockSpec((tm, LANES), row),
        pl.BlockSpec((1, nb, KV_WIDTH), lambda i: (i, 0, 0)),
    )
    in_specs = [pl.BlockSpec((tm, D_MODEL), row), _resident((1, D_MODEL)), _resident(wqT.shape),
                _resident(wkv.shape), _resident(wvT.shape), _resident(wrest.shape), _resident(wdt.shape)]
    return pl.pallas_call(
        _in_proj_kernel, out_shape=out_shape, grid=(n // tm,), in_specs=in_specs, out_specs=out_specs,
        compiler_params=_cparams("parallel"), name="in_proj",
    )(x, norm_g.reshape(1, D_MODEL), wqT, wkv, wvT, wrest, wdt)


def _moba_prompt_kernel(qT_ref, ka_ref, vT_ref, ksum_ref, g_ref, o_ref, qa_ref, m_ref, acc_ref,
                        s0_ref, c0_ref, s1_ref, c1_ref):
    i = pl.program_id(1)
    n_blk = ksum_ref.shape[0]
    kbar = (ksum_ref[...] * (1.0 / MOBA_BLOCK)).astype(BF16)
    blk = lax.broadcasted_iota(jnp.int32, (n_blk, MOBA_BLOCK), 0)
    key_pos = lax.broadcasted_iota(jnp.int32, (MOBA_BLOCK, MOBA_BLOCK), 0)
    qry_pos = lax.broadcasted_iota(jnp.int32, (MOBA_BLOCK, MOBA_BLOCK), 1)

    for h in range(Q_PER_KV):
        cols = slice(h * MOBA_BLOCK, (h + 1) * MOBA_BLOCK)
        qT = qT_ref[h * HEAD_DIM:(h + 1) * HEAD_DIM, :]
        gate = jnp.dot(kbar, qT, preferred_element_type=F32)
        gate = jnp.where(blk < i, gate, -jnp.inf)
        sel = blk < 0
        for r in range(MOBA_TOPK):
            best = jnp.max(gate, axis=0, keepdims=True)
            first = jnp.min(jnp.where(gate == best, blk, n_blk), axis=0, keepdims=True)
            pick = jnp.logical_and(blk == first, r < i)
            sel = jnp.logical_or(sel, pick)
            gate = jnp.where(pick, -jnp.inf, gate)
        qa_ref[:HEAD_DIM, cols] = qT
        qa_ref[HEAD_DIM:HEAD_DIM + n_blk, cols] = jnp.where(sel, 0.0, MASK_BIAS).astype(BF16)
        if n_blk < LANES:
            qa_ref[HEAD_DIM + n_blk:, cols] = jnp.zeros((LANES - n_blk, MOBA_BLOCK), BF16)

    k_own = ka_ref[i][:, :HEAD_DIM]
    for h in range(Q_PER_KV):
        s = jnp.dot(k_own, qT_ref[h * HEAD_DIM:(h + 1) * HEAD_DIM, :], preferred_element_type=F32)
        s = jnp.where(key_pos <= qry_pos, s, MASK_BIAS)
        m_new = _reduce_rows(s, jnp.maximum, jnp.max)
        m_ref[h] = m_new
        acc_ref[h] = jnp.dot(vT_ref[i], jnp.exp2(s - m_new).astype(BF16), preferred_element_type=F32)

    kb = KEY_BLOCKS_PER_STEP
    n_groups = (i + kb - 1) // kb
    last_group = n_groups - 1

    def logits(group, dst_ref, dmax_ref):
        k_aug = ka_ref[pl.ds(kb * group, kb)].reshape(kb * MOBA_BLOCK, 2 * HEAD_DIM)
        for h in range(Q_PER_KV):
            s = jnp.dot(k_aug, qa_ref[:, h * MOBA_BLOCK:(h + 1) * MOBA_BLOCK],
                        preferred_element_type=F32)
            dst_ref[h] = s
            dmax_ref[h] = _reduce_rows(s, jnp.maximum, jnp.max)

    def softmax_step(group, cur_ref, cmax_ref, nxt_ref, nmax_ref):
        logits(jnp.minimum(group + 1, last_group), nxt_ref, nmax_ref)
        vT = jnp.concatenate([vT_ref[kb * group + b] for b in range(kb)], axis=1)
        for h in range(Q_PER_KV):
            m_old = m_ref[h]
            m_new = jnp.maximum(m_old, cmax_ref[h])
            p = jnp.exp2(cur_ref[h] - m_new).astype(BF16)
            acc_ref[h] = jnp.exp2(m_old - m_new) * acc_ref[h] + jnp.dot(vT, p, preferred_element_type=F32)
            m_ref[h] = m_new

    @pl.when(n_groups > 0)
    def _():
        logits(0, s0_ref, c0_ref)

    def one_group(group, carry):
        @pl.when(group % 2 == 0)
        def _():
            softmax_step(group, s0_ref, c0_ref, s1_ref, c1_ref)

        @pl.when(group % 2 == 1)
        def _():
            softmax_step(group, s1_ref, c1_ref, s0_ref, c0_ref)

        return carry

    lax.fori_loop(0, n_groups, one_group, 0)

    for h in range(Q_PER_KV):
        acc = acc_ref[h]
        out = (acc[:HEAD_DIM] / acc[HEAD_DIM:HEAD_DIM + 1]).T
        dcols = slice(h * HEAD_DIM, (h + 1) * HEAD_DIM)
        o_ref[:, dcols] = (out * _silu(g_ref[:, dcols])).astype(BF16)


def _moba_prompt(qT, ka, vT, ksum, gz):
    n = qT.shape[1]
    n_blk = n // MOBA_BLOCK
    assert n_blk <= LANES and n_blk % KEY_BLOCKS_PER_STEP == 0
    kv_rows = Q_PER_KV * HEAD_DIM
    logit_buf = pltpu.VMEM((Q_PER_KV, KEY_BLOCKS_PER_STEP * MOBA_BLOCK, MOBA_BLOCK), F32)
    stat_buf = pltpu.VMEM((Q_PER_KV, 1, MOBA_BLOCK), F32)
    return pl.pallas_call(
        _moba_prompt_kernel,
        out_shape=jax.ShapeDtypeStruct((n, ATT_WIDTH), BF16),
        grid=(N_KV_HEADS, n_blk),
        in_specs=[
            pl.BlockSpec((kv_rows, MOBA_BLOCK), lambda h, i: (h, i)),
            pl.BlockSpec((None, n_blk, MOBA_BLOCK, 2 * HEAD_DIM), lambda h, i: (h, 0, 0, 0)),
            pl.BlockSpec((n_blk, VT_ROWS, MOBA_BLOCK), lambda h, i: (0, h, 0)),
            pl.BlockSpec((n_blk, HEAD_DIM), lambda h, i: (0, h)),
            pl.BlockSpec((MOBA_BLOCK, kv_rows), lambda h, i: (i, h)),
        ],
        out_specs=pl.BlockSpec((MOBA_BLOCK, kv_rows), lambda h, i: (i, h)),
        scratch_shapes=[pltpu.VMEM((2 * HEAD_DIM, Q_PER_KV * MOBA_BLOCK), BF16),
                        stat_buf,
                        pltpu.VMEM((Q_PER_KV, VT_ROWS, MOBA_BLOCK), F32),
                        logit_buf, stat_buf, logit_buf, stat_buf],
        compiler_params=_cparams("parallel", "parallel"), name="moba_prompt",
    )(qT, ka, vT, ksum, gz)


def _ssd_prompt_kernel(xbc_ref, z_ref, dt_ref, convw_ref, convb_ref, dtb_ref, acoef_ref, dskip_ref, ng_ref,
                       o_ref, state_ref, xpad_ref, st_ref, y_ref):
    c = pl.program_id(0)
    t = SSD_CHUNK

    @pl.when(c == 0)
    def _():
        xpad_ref[:SUBLANES, :] = jnp.zeros((SUBLANES, CONV_DIM), F32)
        st_ref[...] = jnp.zeros_like(st_ref)

    xpad_ref[SUBLANES:, :] = xbc_ref[...]
    conv = convb_ref[...] + convw_ref[CONV_WIDTH - 1:CONV_WIDTH, :] * xpad_ref[SUBLANES:, :]
    for w in range(CONV_WIDTH - 1):
        back = CONV_WIDTH - 1 - w
        conv = conv + convw_ref[w:w + 1, :] * xpad_ref[SUBLANES - back:SUBLANES - back + t, :]
    xpad_ref[:SUBLANES, :] = xpad_ref[t:t + SUBLANES, :]
    xc = _silu(conv)
    xs = xc[:, :SSD_WIDTH]

    dt = jax.nn.softplus(dt_ref[...] + dtb_ref[...])
    a = dt * acoef_ref[...]
    row = lax.broadcasted_iota(jnp.int32, (t, t), 0)
    colm = lax.broadcasted_iota(jnp.int32, (t, t), 1)
    lower = row >= colm
    tril = lower.astype(BF16)
    a_hi, a_mid, a_lo = _split3(a)
    a_cum = ((jnp.dot(tril, a_lo, preferred_element_type=F32) + jnp.dot(tril, a_mid, preferred_element_type=F32))
             + jnp.dot(tril, a_hi, preferred_element_type=F32))
    a_cum_t = a_cum.T
    a_end = a_cum[t - 1:t, :]

    for g in range(SSD_GROUPS):
        b_g = xc[:, SSD_WIDTH + g * D_STATE:SSD_WIDTH + (g + 1) * D_STATE]
        c_g = xc[:, SSD_WIDTH + (SSD_GROUPS + g) * D_STATE:SSD_WIDTH + (SSD_GROUPS + g + 1) * D_STATE]
        b_bf = b_g.astype(BF16)
        c_bf = c_g.astype(BF16)
        cb = lax.dot_general(c_bf, b_bf, NT_DIMS, preferred_element_type=F32)
        for hh in range(HEADS_PER_GROUP):
            h = g * HEADS_PER_GROUP + hh
            hc = slice(h * SSD_HEAD_DIM, (h + 1) * SSD_HEAD_DIM)
            acol = a_cum[:, h:h + 1]
            arow = a_cum_t[h:h + 1, :]
            xs_h = xs[:, hc]
            xdt = xs_h * dt[:, h:h + 1]
            decay = jnp.exp(jnp.where(lower, acol - arow, -jnp.inf))
            y = jnp.dot((cb * decay).astype(BF16), xdt.astype(BF16), preferred_element_type=F32)
            st = st_ref[h]
            y = y + lax.dot_general(c_bf, st.astype(BF16), NT_DIMS, preferred_element_type=F32) * jnp.exp(acol)
            xdtd = xdt * jnp.exp(a_end[:, h:h + 1] - acol)
            st_ref[h] = st * jnp.exp(a_end[:, h:h + 1]) + lax.dot_general(
                xdtd.astype(BF16), b_bf, TN_DIMS, preferred_element_type=F32)
            y_ref[:, hc] = y + dskip_ref[:, h:h + 1] * xs_h

    y = y_ref[...] * _silu(z_ref[...])
    o_ref[...] = _rmsnorm(y, ng_ref[...]).astype(o_ref.dtype)

    @pl.when(c == pl.num_programs(0) - 1)
    def _():
        state_ref[...] = st_ref[...]


def _ssd_prompt(xbc, gz, dt_raw, conv_w, conv_b, dt_bias, a_log, d_skip, ssd_norm_g):
    n = xbc.shape[0]
    t = SSD_CHUNK
    assert n % t == 0
    pad = lambda v: jnp.pad(v.reshape(1, N_SSD_HEADS), ((0, 0), (0, LANES - N_SSD_HEADS)))
    a_coef = -jnp.exp(a_log.astype(F32))
    row = lambda c: (c, 0)
    return pl.pallas_call(
        _ssd_prompt_kernel,
        out_shape=(jax.ShapeDtypeStruct((n, SSD_WIDTH), BF16),
                   jax.ShapeDtypeStruct((N_SSD_HEADS, SSD_HEAD_DIM, D_STATE), F32)),
        grid=(n // t,),
        in_specs=[pl.BlockSpec((t, CONV_DIM), row),
                  pl.BlockSpec((t, SSD_WIDTH), lambda c: (c, 1)),
                  pl.BlockSpec((t, LANES), row),
                  _resident((CONV_WIDTH, CONV_DIM)), _resident((1, CONV_DIM)), _resident((1, LANES)),
                  _resident((1, LANES)), _resident((1, LANES)), _resident((1, SSD_WIDTH))],
        out_specs=(pl.BlockSpec((t, SSD_WIDTH), row),
                   pl.BlockSpec((N_SSD_HEADS, SSD_HEAD_DIM, D_STATE), lambda c: (0, 0, 0))),
        scratch_shapes=[pltpu.VMEM((t + SUBLANES, CONV_DIM), F32),
                        pltpu.VMEM((N_SSD_HEADS, SSD_HEAD_DIM, D_STATE), F32),
                        pltpu.VMEM((t, SSD_WIDTH), F32)],
        compiler_params=_cparams("arbitrary"), name="ssd_prompt",
    )(xbc, gz, dt_raw, conv_w, conv_b.reshape(1, CONV_DIM), pad(dt_bias), pad(a_coef), pad(d_skip),
      ssd_norm_g.reshape(1, SSD_WIDTH))


def _out_proj_kernel(x_ref, oa_ref, os_ref, w_ref, fg_ref, h_ref, *, final_norm):
    out = jnp.dot(oa_ref[...], w_ref[:ATT_WIDTH, :], preferred_element_type=F32)
    out = out + jnp.dot(os_ref[...], w_ref[ATT_WIDTH:, :], preferred_element_type=F32)
    h = x_ref[...] + out
    h_ref[...] = _rmsnorm(h, fg_ref[...]) if final_norm else h


def _out_proj(x, o_att, o_ssd, w_out_bf, final_g, final_norm, tm):
    n = x.shape[0]
    assert n % tm == 0
    row = lambda i: (i, 0)
    return pl.pallas_call(
        functools.partial(_out_proj_kernel, final_norm=final_norm),
        out_shape=jax.ShapeDtypeStruct((n, D_MODEL), F32),
        grid=(n // tm,),
        in_specs=[pl.BlockSpec((tm, D_MODEL), row), pl.BlockSpec((tm, ATT_WIDTH), row),
                  pl.BlockSpec((tm, SSD_WIDTH), row), _resident((MIX_WIDTH, D_MODEL)), _resident((1, D_MODEL))],
        out_specs=pl.BlockSpec((tm, D_MODEL), row),
        compiler_params=_cparams("parallel"), name="out_proj",
    )(x, o_att, o_ssd, w_out_bf, final_g.reshape(1, D_MODEL))


def _prompt_layer(x, norm_g, w_in, conv_w, conv_b, dt_bias, a_log, d_skip, ssd_norm_g, w_out, final_g,
                  final_norm, tm):
    n = x.shape[0]
    qT, k, v, ka, vT, gz, xbc, dt_raw, ksum = _in_proj(x, norm_g, _in_proj_weights(w_in), tm)
    o_att = _moba_prompt(qT, ka, vT, ksum.reshape(n // MOBA_BLOCK, KV_WIDTH), gz)
    o_ssd, state = _ssd_prompt(xbc, gz, dt_raw, conv_w, conv_b, dt_bias, a_log, d_skip, ssd_norm_g)
    h = _out_proj(x, o_att, o_ssd, w_out.astype(BF16), final_g, final_norm, tm)
    return h, k, v, xbc[n - (CONV_WIDTH - 1):], state


PAGES_PER_BLOCK = MOBA_BLOCK // PAGE_SIZE
PAGE_ROWS = PAGE_SIZE * N_KV_HEADS
SAMPLE_U_WIDTH = 49 * LANES


def _in_proj_sample_kernel(x_ref, g_ref, w_ref, u_ref):
    xn = _rmsnorm(x_ref[...], g_ref[...]).astype(BF16)
    u_ref[...] = jnp.dot(xn, w_ref[...], preferred_element_type=F32)


def _in_proj_sample(x, norm_g, w_in):
    n = x.shape[0]
    wb = w_in.astype(BF16)
    w = jnp.concatenate([wb, jnp.repeat(wb[:, XBC_END:], SSD_HEAD_DIM, axis=1)], axis=1)
    w = jnp.pad(w, ((0, 0), (0, SAMPLE_U_WIDTH - w.shape[1])))
    return pl.pallas_call(
        _in_proj_sample_kernel, out_shape=jax.ShapeDtypeStruct((n, SAMPLE_U_WIDTH), F32), grid=(1,),
        in_specs=[_resident((n, D_MODEL)), _resident((1, D_MODEL)), _resident(w.shape)],
        out_specs=pl.BlockSpec((n, SAMPLE_U_WIDTH), lambda i: (0, 0)),
        compiler_params=_cparams("arbitrary"), name="in_proj_sample",
    )(x, norm_g.reshape(1, D_MODEL), w)


def _kbar_sample_kernel(pt_ref, cache_ref, out_ref, buf_ref, sem_ref, *, layer):
    b = pl.program_id(0)
    n_pages = pt_ref.shape[1]
    slot = b % 2

    def copies(seq, slot):
        return [pltpu.make_async_copy(cache_ref.at[layer, pt_ref[seq, p]], buf_ref.at[slot, p], sem_ref.at[slot])
                for p in range(n_pages)]

    @pl.when(b == 0)
    def _():
        for cp in copies(0, 0):
            cp.start()

    @pl.when(b + 1 < pl.num_programs(0))
    def _():
        for cp in copies(b + 1, 1 - slot):
            cp.start()

    for cp in copies(b, slot):
        cp.wait()
    for blk in range(n_pages // PAGES_PER_BLOCK):
        acc = jnp.zeros((SUBLANES, HEAD_DIM), F32)
        for p in range(PAGES_PER_BLOCK):
            page = buf_ref[slot, blk * PAGES_PER_BLOCK + p]
            acc = acc + jnp.sum(page.reshape(PAGE_ROWS // SUBLANES, SUBLANES, HEAD_DIM), axis=0)
        acc = acc + pltpu.roll(acc, 2, 0)
        acc = acc + pltpu.roll(acc, 4, 0)
        for h in range(N_KV_HEADS):
            out_ref[0, h, blk:blk + 1, :] = acc[h:h + 1, :] * (1.0 / MOBA_BLOCK)


def _kbar_sample(cache_pages, page_table, layer):
    n_seq, n_pages = page_table.shape
    assert n_pages % PAGES_PER_BLOCK == 0
    n_blk = n_pages // PAGES_PER_BLOCK
    return pl.pallas_call(
        functools.partial(_kbar_sample_kernel, layer=layer),
        out_shape=jax.ShapeDtypeStruct((n_seq, N_KV_HEADS, n_blk, HEAD_DIM), F32),
        grid_spec=pltpu.PrefetchScalarGridSpec(
            num_scalar_prefetch=1, grid=(n_seq,),
            in_specs=[pl.BlockSpec(memory_space=pl.ANY)],
            out_specs=pl.BlockSpec((1, N_KV_HEADS, n_blk, HEAD_DIM), lambda b, pt: (b, 0, 0, 0)),
            scratch_shapes=[pltpu.VMEM((2, n_pages, PAGE_ROWS, HEAD_DIM), F32),
                            pltpu.SemaphoreType.DMA((2,))]),
        compiler_params=_cparams("arbitrary"), name="kbar_sample",
    )(page_table, cache_pages)


def _gate_sample_kernel(q_ref, kbar_ref, idx_ref, gate_ref):
    n_seq, _, n_blk, _ = kbar_ref.shape
    for b in range(n_seq):
        for kh in range(N_KV_HEADS):
            r0 = b * N_ATT_HEADS + kh * Q_PER_KV
            q4 = q_ref[r0:r0 + Q_PER_KV, :].astype(BF16)
            gate_ref[r0:r0 + Q_PER_KV, :] = lax.dot_general(
                q4, kbar_ref[b, kh].astype(BF16), NT_DIMS, preferred_element_type=F32)
    gate = gate_ref[...]
    lane = lax.broadcasted_iota(jnp.int32, gate.shape, 1)
    idx_ref[...] = jnp.zeros_like(idx_ref)
    for r in range(MOBA_TOPK):
        best = jnp.max(gate, axis=1, keepdims=True)
        first = jnp.min(jnp.where(gate == best, lane, n_blk), axis=1, keepdims=True)
        idx_ref[:, r:r + 1] = first
        gate = jnp.where(lane == first, -jnp.inf, gate)


def _gate_sample(q_rows, kbar):
    n_seq, _, n_blk, _ = kbar.shape
    assert n_blk >= MOBA_TOPK
    rows = n_seq * N_ATT_HEADS
    return pl.pallas_call(
        _gate_sample_kernel, out_shape=jax.ShapeDtypeStruct((rows, LANES), jnp.int32), grid=(1,),
        in_specs=[_resident((rows, HEAD_DIM)), _resident(kbar.shape)],
        out_specs=pl.BlockSpec((rows, LANES), lambda i: (0, 0)),
        scratch_shapes=[pltpu.VMEM((rows, n_blk), F32)],
        compiler_params=_cparams("arbitrary"), name="gate_sample",
    )(q_rows, kbar)


def _attn_sample_kernel(pt_ref, idx_ref, q_ref, knew_ref, vnew_ref, g_ref, ck_ref, cv_ref, o_ref,
                        kbuf_ref, vbuf_ref, sem_ref, *, layer):
    b = pl.program_id(0)
    n_sel = N_ATT_HEADS * MOBA_TOPK
    cur = b % 2

    def copies(seq, buf):
        out = []
        for s in range(n_sel):
            blk = idx_ref[seq * n_sel + s]
            for p in range(PAGES_PER_BLOCK):
                page = pt_ref[seq, blk * PAGES_PER_BLOCK + p]
                slot = s * PAGES_PER_BLOCK + p
                out.append(pltpu.make_async_copy(ck_ref.at[layer, page], kbuf_ref.at[buf, slot], sem_ref.at[buf, 0]))
                out.append(pltpu.make_async_copy(cv_ref.at[layer, page], vbuf_ref.at[buf, slot], sem_ref.at[buf, 1]))
        return out

    @pl.when(b == 0)
    def _():
        for cp in copies(0, 0):
            cp.start()

    @pl.when(b + 1 < pl.num_programs(0))
    def _():
        for cp in copies(b + 1, 1 - cur):
            cp.start()

    for cp in copies(b, cur):
        cp.wait()

    row_head = lax.broadcasted_iota(jnp.int32, (1, PAGE_ROWS), 1) % N_KV_HEADS
    for h in range(N_ATT_HEADS):
        kh = h // Q_PER_KV
        qh = q_ref[0, h:h + 1, :].astype(BF16)
        q8 = jnp.broadcast_to(qh, (SUBLANES, HEAD_DIM))
        valid = row_head == kh
        scores = []
        for s in range(MOBA_TOPK * PAGES_PER_BLOCK):
            page = kbuf_ref[cur, h * MOBA_TOPK * PAGES_PER_BLOCK + s].astype(BF16)
            sc = lax.dot_general(q8, page, NT_DIMS, preferred_element_type=F32)[0:1, :]
            scores.append(jnp.where(valid, sc, -jnp.inf))
        k_self = knew_ref[0, kh:kh + 1, :].astype(BF16).astype(F32)
        s_self = jnp.sum(qh.astype(F32) * k_self, axis=1, keepdims=True)
        m = s_self
        for sc in scores:
            m = jnp.maximum(m, jnp.max(sc, axis=1, keepdims=True))
        p_self = jnp.exp((s_self - m) * ATT_SCALE)
        l = p_self
        acc = p_self.astype(BF16).astype(F32) * vnew_ref[0, kh:kh + 1, :].astype(BF16).astype(F32)
        for s, sc in enumerate(scores):
            p = jnp.exp((sc - m) * ATT_SCALE)
            l = l + jnp.sum(p, axis=1, keepdims=True)
            p8 = jnp.broadcast_to(p.astype(BF16), (SUBLANES, PAGE_ROWS))
            page = vbuf_ref[cur, h * MOBA_TOPK * PAGES_PER_BLOCK + s].astype(BF16)
            acc = acc + jnp.dot(p8, page, preferred_element_type=F32)[0:1, :]
        o_ref[0, h:h + 1, :] = ((acc / l) * _silu(g_ref[0, h:h + 1, :])).astype(o_ref.dtype)


def _attn_sample(q3, knew3, vnew3, g3, cache_k_pages, cache_v_pages, page_table, idx_flat, layer):
    n_seq = q3.shape[0]
    n_slots = N_ATT_HEADS * MOBA_TOPK * PAGES_PER_BLOCK
    per_seq = lambda r: pl.BlockSpec((1, r, HEAD_DIM), lambda b, pt, ix: (b, 0, 0))
    return pl.pallas_call(
        functools.partial(_attn_sample_kernel, layer=layer),
        out_shape=jax.ShapeDtypeStruct((n_seq, N_ATT_HEADS, HEAD_DIM), BF16),
        grid_spec=pltpu.PrefetchScalarGridSpec(
            num_scalar_prefetch=2, grid=(n_seq,),
            in_specs=[per_seq(N_ATT_HEADS), per_seq(N_KV_HEADS), per_seq(N_KV_HEADS), per_seq(N_ATT_HEADS),
                      pl.BlockSpec(memory_space=pl.ANY), pl.BlockSpec(memory_space=pl.ANY)],
            out_specs=per_seq(N_ATT_HEADS),
            scratch_shapes=[pltpu.VMEM((2, n_slots, PAGE_ROWS, HEAD_DIM), F32),
                            pltpu.VMEM((2, n_slots, PAGE_ROWS, HEAD_DIM), F32),
                            pltpu.SemaphoreType.DMA((2, 2))]),
        compiler_params=_cparams("arbitrary"), name="attn_sample",
    )(page_table, idx_flat, q3, knew3, vnew3, g3, cache_k_pages, cache_v_pages)


def _column(row):
    return jnp.broadcast_to(row, (LANES, LANES)).T


def _ssd_sample_kernel(xbc_ref, z_ref, dt_ref, sc_ref, ss_ref, convw_ref, convb_ref, dtb_ref, acoef_ref,
                       dskip_ref, ng_ref, o_ref, nc_ref, ns_ref, y_ref):
    hist = sc_ref[0]
    new = xbc_ref[0]
    conv = convb_ref[...] + convw_ref[CONV_WIDTH - 1:CONV_WIDTH, :] * new
    for w in range(CONV_WIDTH - 1):
        conv = conv + convw_ref[w:w + 1, :] * hist[w:w + 1, :]
    nc_ref[0, :CONV_WIDTH - 2, :] = hist[1:, :]
    nc_ref[0, CONV_WIDTH - 2:, :] = new
    xc = _silu(conv)
    xs = xc[:, :SSD_WIDTH]
    dt = jax.nn.softplus(dt_ref[0] + dtb_ref[...])
    d_a = jnp.exp(dt * acoef_ref[...])
    xdt = xs * dt
    heads_per_chunk = LANES // SSD_HEAD_DIM
    for c in range(SSD_WIDTH // LANES):
        lanes = slice(c * LANES, (c + 1) * LANES)
        x_col = _column(xdt[:, lanes])
        da_col = _column(d_a[:, lanes])
        y_cols = []
        for hh in range(heads_per_chunk):
            h = c * heads_per_chunk + hh
            g = h // HEADS_PER_GROUP
            rows = slice(hh * SSD_HEAD_DIM, (hh + 1) * SSD_HEAD_DIM)
            b_g = xc[:, SSD_WIDTH + g * D_STATE:SSD_WIDTH + (g + 1) * D_STATE]
            c_g = xc[:, SSD_WIDTH + (SSD_GROUPS + g) * D_STATE:SSD_WIDTH + (SSD_GROUPS + g + 1) * D_STATE]
            st = ss_ref[0, h] * da_col[rows, :] + x_col[rows, :] * b_g
            ns_ref[0, h] = st
            y_cols.append(jnp.sum(st * c_g, axis=1, keepdims=True))
        y_col = jnp.concatenate(y_cols, axis=0)
        y_ref[:, lanes] = jnp.broadcast_to(y_col, (LANES, LANES)).T[0:1, :]
    y = (y_ref[...] + dskip_ref[...] * xs) * _silu(z_ref[0])
    o_ref[0] = _rmsnorm(y, ng_ref[...]).astype(o_ref.dtype)


def _ssd_sample(xbc3, z3, dte3, state_conv, state_ssm, layer, conv_w, conv_b, dt_bias, a_log, d_skip, ssd_norm_g):
    n_seq = xbc3.shape[0]
    per_channel = lambda v: jnp.repeat(v.astype(F32), SSD_HEAD_DIM).reshape(1, SSD_WIDTH)
    seq3 = lambda w: pl.BlockSpec((1, 1, w), lambda b: (b, 0, 0))
    return pl.pallas_call(
        _ssd_sample_kernel,
        out_shape=(jax.ShapeDtypeStruct((n_seq, 1, SSD_WIDTH), BF16),
                   jax.ShapeDtypeStruct((n_seq, CONV_WIDTH - 1, CONV_DIM), F32),
                   jax.ShapeDtypeStruct((n_seq, N_SSD_HEADS, SSD_HEAD_DIM, D_STATE), F32)),
        grid=(n_seq,),
        in_specs=[seq3(CONV_DIM), seq3(SSD_WIDTH), seq3(SSD_WIDTH),
                  pl.BlockSpec((None, 1, CONV_WIDTH - 1, CONV_DIM), lambda b: (layer, b, 0, 0)),
                  pl.BlockSpec((None, 1, N_SSD_HEADS, SSD_HEAD_DIM, D_STATE), lambda b: (layer, b, 0, 0, 0)),
                  _resident((CONV_WIDTH, CONV_DIM)), _resident((1, CONV_DIM)), _resident((1, SSD_WIDTH)),
                  _resident((1, SSD_WIDTH)), _resident((1, SSD_WIDTH)), _resident((1, SSD_WIDTH))],
        out_specs=(seq3(SSD_WIDTH),
                   pl.BlockSpec((1, CONV_WIDTH - 1, CONV_DIM), lambda b: (b, 0, 0)),
                   pl.BlockSpec((1, N_SSD_HEADS, SSD_HEAD_DIM, D_STATE), lambda b: (b, 0, 0, 0))),
        scratch_shapes=[pltpu.VMEM((1, SSD_WIDTH), F32)],
        compiler_params=_cparams("parallel"), name="ssd_sample",
    )(xbc3, z3, dte3, state_conv, state_ssm, conv_w, conv_b.reshape(1, CONV_DIM), per_channel(dt_bias),
      per_channel(-jnp.exp(a_log.astype(F32))), per_channel(d_skip), ssd_norm_g.reshape(1, SSD_WIDTH))


def _sample_layer(x, layer, cache_k_pages, cache_v_pages, state_conv, state_ssm, page_table,
                  norm_g, w_in, conv_w, conv_b, dt_bias, a_log, d_skip, ssd_norm_g, w_out, final_g, final_norm):
    n_seq = x.shape[0]
    u = _in_proj_sample(x, norm_g, w_in)
    q, k, v, g_att = u[:, :Q_END], u[:, Q_END:K_END], u[:, K_END:V_END], u[:, V_END:G_END]
    z, xbc, dte = u[:, G_END:Z_END], u[:, Z_END:XBC_END], u[:, IN_WIDTH:IN_WIDTH + SSD_WIDTH]
    kbar = _kbar_sample(cache_k_pages, page_table, layer)
    idx = _gate_sample(q.reshape(n_seq * N_ATT_HEADS, HEAD_DIM), kbar)
    heads = lambda a, r: a.reshape(n_seq, r, HEAD_DIM)
    o_att = _attn_sample(heads(q, N_ATT_HEADS), heads(k, N_KV_HEADS), heads(v, N_KV_HEADS),
                         heads(g_att, N_ATT_HEADS), cache_k_pages, cache_v_pages, page_table,
                         idx[:, :MOBA_TOPK].reshape(-1), layer)
    rows3 = lambda a: a.reshape(n_seq, 1, a.shape[1])
    o_ssd, new_conv, new_ssm = _ssd_sample(rows3(xbc), rows3(z), rows3(dte), state_conv, state_ssm, layer,
                                           conv_w, conv_b, dt_bias, a_log, d_skip, ssd_norm_g)
    h = _out_proj(x, o_att.reshape(n_seq, ATT_WIDTH), o_ssd.reshape(n_seq, SSD_WIDTH), w_out.astype(BF16),
                  final_g, final_norm, n_seq)
    return h, k, v, new_conv, new_ssm


PROMPT_ROW_TILE = 512


def kernel(x_prompt, x_sample, cache_k, cache_v, state_conv, state_ssm, page_table, norm_g, w_in, conv_w,
           conv_b, dt_bias, a_log, d_skip, ssd_norm_g, w_out, final_norm_g):
    depth = w_in.shape[0]
    batch, seq, _ = x_prompt.shape
    n_seq, dec_seq, _ = x_sample.shape
    assert batch == 1 and dec_seq == 1
    n_phys = cache_k.shape[1]
    ck = cache_k.reshape(depth, n_phys, PAGE_ROWS, HEAD_DIM)
    cv = cache_v.reshape(depth, n_phys, PAGE_ROWS, HEAD_DIM)
    hp, hs = x_prompt[0], x_sample[:, 0]
    outs = [[] for _ in range(8)]
    for l in range(depth):
        last = l == depth - 1
        params = (norm_g[l], w_in[l], conv_w[l], conv_b[l], dt_bias[l], a_log[l], d_skip[l], ssd_norm_g[l],
                  w_out[l], final_norm_g, last)
        hp, kp, vp, cp, sp = _prompt_layer(hp, *params, PROMPT_ROW_TILE)
        hs, ks, vs, cs, ss = _sample_layer(hs, l, ck, cv, state_conv, state_ssm, page_table, *params)
        kv4 = lambda a, n: a.reshape(n, -1, N_KV_HEADS, HEAD_DIM)
        for lst, val in zip(outs, (kv4(kp, batch), kv4(vp, batch), cp[None], sp[None],
                                   kv4(ks, n_seq), kv4(vs, n_seq), cs, ss)):
            lst.append(val)
    return (hp[None], hs[:, None]) + tuple(jnp.stack(o) for o in outs)
```

```python
import functools

import jax
import jax.numpy as jnp
from jax import lax
from jax.experimental import pallas as pl
from jax.experimental.pallas import tpu as pltpu

F32 = jnp.float32
BF16 = jnp.bfloat16

D_MODEL = 1024
N_ATT_HEADS = 8
N_KV_HEADS = 2
HEAD_DIM = 128
Q_PER_KV = N_ATT_HEADS // N_KV_HEADS
ATT_WIDTH = N_ATT_HEADS * HEAD_DIM
KV_WIDTH = N_KV_HEADS * HEAD_DIM
MOBA_BLOCK = 256
MOBA_TOPK = 3
PAGE_SIZE = 128
SSD_WIDTH = 1024
SSD_HEAD_DIM = 64
N_SSD_HEADS = SSD_WIDTH // SSD_HEAD_DIM
SSD_GROUPS = 2
HEADS_PER_GROUP = N_SSD_HEADS // SSD_GROUPS
GROUP_WIDTH = HEADS_PER_GROUP * SSD_HEAD_DIM
D_STATE = 128
CONV_WIDTH = 4
SSD_CHUNK = 256
CONV_DIM = SSD_WIDTH + 2 * SSD_GROUPS * D_STATE
MIX_WIDTH = ATT_WIDTH + SSD_WIDTH
Q_END = ATT_WIDTH
K_END = Q_END + KV_WIDTH
V_END = K_END + KV_WIDTH
G_END = V_END + ATT_WIDTH
Z_END = G_END + SSD_WIDTH
XBC_END = Z_END + CONV_DIM
IN_WIDTH = XBC_END + N_SSD_HEADS
NORM_EPS = 1e-5
ATT_SCALE = HEAD_DIM ** -0.5
EXP2_SCALE = ATT_SCALE * 1.4426950408889634

LANES = 128
SUBLANES = 8
VMEM_LIMIT_BYTES = 56 * 1024 * 1024

MASK_BIAS = -1e30
VT_ONES = 16
VT_ROWS = HEAD_DIM + VT_ONES
KEY_BLOCKS_PER_STEP = 2

NT_DIMS = (((1,), (1,)), ((), ()))
TN_DIMS = (((0,), (0,)), ((), ()))


def _cparams(*semantics):
    return pltpu.CompilerParams(dimension_semantics=semantics, vmem_limit_bytes=VMEM_LIMIT_BYTES)


def _resident(shape):
    nd = len(shape)
    return pl.BlockSpec(shape, lambda *_: (0,) * nd, pipeline_mode=pl.Buffered(1))


def _rmsnorm(x, g):
    return x * lax.rsqrt(jnp.mean(x * x, axis=-1, keepdims=True) + NORM_EPS) * g


def _silu(x):
    return x * jax.nn.sigmoid(x)


def _reduce_rows(x, op, final):
    slabs = [x[r:r + SUBLANES] for r in range(0, x.shape[0], SUBLANES)]
    while len(slabs) > 1:
        nxt = [op(slabs[a], slabs[a + 1]) for a in range(0, len(slabs) - 1, 2)]
        if len(slabs) % 2:
            nxt.append(slabs[-1])
        slabs = nxt
    return final(slabs[0], axis=0, keepdims=True)


def _split3(x):
    hi = x.astype(BF16)
    r = x - hi.astype(F32)
    mid = r.astype(BF16)
    lo = (r - mid.astype(F32)).astype(BF16)
    return hi, mid, lo


def _dot3(parts, m, left=False):
    if left:
        f = lambda p: jnp.dot(m, p, preferred_element_type=F32)
    else:
        f = lambda p: jnp.dot(p, m, preferred_element_type=F32)
    return (f(parts[2]) + f(parts[1])) + f(parts[0])


def _in_proj_kernel(x_ref, g_ref, wqT_ref, wkv_ref, wvT_ref, wrest_ref, wdt_ref,
                    qT_ref, k_ref, v_ref, ka_ref, vT_ref, gz_ref, xbc_ref, dt_ref, ksum_ref):
    tm = x_ref.shape[0]
    xn = _rmsnorm(x_ref[...], g_ref[...]).astype(BF16)
    qT = lax.dot_general(wqT_ref[...], xn, NT_DIMS, preferred_element_type=F32)
    qT_ref[...] = (qT * EXP2_SCALE).astype(BF16)
    vT = lax.dot_general(wvT_ref[...], xn, NT_DIMS, preferred_element_type=F32).astype(BF16)
    kv = jnp.dot(xn, wkv_ref[...], preferred_element_type=F32)
    k = kv[:, :KV_WIDTH]
    for h in range(N_KV_HEADS):
        k_ref[:, h, :] = kv[:, h * HEAD_DIM:(h + 1) * HEAD_DIM]
        v_ref[:, h, :] = kv[:, KV_WIDTH + h * HEAD_DIM:KV_WIDTH + (h + 1) * HEAD_DIM]
    blk0 = pl.program_id(0) * (tm // MOBA_BLOCK)
    lane = lax.broadcasted_iota(jnp.int32, (MOBA_BLOCK, LANES), 1)
    for b in range(tm // MOBA_BLOCK):
        rows = slice(b * MOBA_BLOCK, (b + 1) * MOBA_BLOCK)
        onehot = (lane == blk0 + b).astype(BF16)
        for h in range(N_KV_HEADS):
            ka_ref[h, b, :, :HEAD_DIM] = k[rows, h * HEAD_DIM:(h + 1) * HEAD_DIM].astype(BF16)
            ka_ref[h, b, :, HEAD_DIM:] = onehot
        for h in range(N_KV_HEADS):
            vT_ref[b, h * VT_ROWS:h * VT_ROWS + HEAD_DIM, :] = vT[h * HEAD_DIM:(h + 1) * HEAD_DIM, rows]
            vT_ref[b, h * VT_ROWS + HEAD_DIM:(h + 1) * VT_ROWS, :] = jnp.ones((VT_ONES, MOBA_BLOCK), BF16)
        ksum_ref[0, b:b + 1, :] = jnp.sum(k[rows], axis=0, keepdims=True)
    rest = jnp.dot(xn, wrest_ref[...], preferred_element_type=F32)
    gz_ref[...] = rest[:, :ATT_WIDTH + SSD_WIDTH]
    xbc_ref[...] = rest[:, ATT_WIDTH + SSD_WIDTH:]
    dt_ref[...] = jnp.dot(xn, wdt_ref[...], preferred_element_type=F32)


def _in_proj_weights(w_in):
    wb = w_in.astype(BF16)
    wqT = wb[:, :Q_END].T
    wkv = wb[:, Q_END:V_END]
    wvT = wb[:, K_END:V_END].T
    wrest = wb[:, V_END:XBC_END]
    wdt = jnp.pad(wb[:, XBC_END:], ((0, 0), (0, LANES - N_SSD_HEADS)))
    return wqT, wkv, wvT, wrest, wdt


def _in_proj(x, norm_g, weights, tm):
    n = x.shape[0]
    assert n % tm == 0 and tm % MOBA_BLOCK == 0
    wqT, wkv, wvT, wrest, wdt = weights
    nb = tm // MOBA_BLOCK
    row = lambda i: (i, 0)
    kv_spec = pl.BlockSpec((tm, N_KV_HEADS, HEAD_DIM), lambda i: (i, 0, 0))
    out_shape = (
        jax.ShapeDtypeStruct((ATT_WIDTH, n), BF16),
        jax.ShapeDtypeStruct((n, N_KV_HEADS, HEAD_DIM), F32),
        jax.ShapeDtypeStruct((n, N_KV_HEADS, HEAD_DIM), F32),
        jax.ShapeDtypeStruct((N_KV_HEADS, n // MOBA_BLOCK, MOBA_BLOCK, 2 * HEAD_DIM), BF16),
        jax.ShapeDtypeStruct((n // MOBA_BLOCK, N_KV_HEADS * VT_ROWS, MOBA_BLOCK), BF16),
        jax.ShapeDtypeStruct((n, ATT_WIDTH + SSD_WIDTH), F32),
        jax.ShapeDtypeStruct((n, CONV_DIM), F32),
        jax.ShapeDtypeStruct((n, LANES), F32),
        jax.ShapeDtypeStruct((n // tm, nb, KV_WIDTH), F32),
    )
    out_specs = (
        pl.BlockSpec((ATT_WIDTH, tm), lambda i: (0, i)),
        kv_spec,
        kv_spec,
        pl.BlockSpec((N_KV_HEADS, nb, MOBA_BLOCK, 2 * HEAD_DIM), lambda i: (0, i, 0, 0)),
        pl.BlockSpec((nb, N_KV_HEADS * VT_ROWS, MOBA_BLOCK), lambda i: (i, 0, 0)),
        pl.BlockSpec((tm, ATT_WIDTH + SSD_WIDTH), row),
        pl.BlockSpec((tm, CONV_DIM), row),
        pl.BlockSpec((tm, LANES), row),
        pl.BlockSpec((1, nb, KV_WIDTH), lambda i: (i, 0, 0)),
    )
    in_specs = [pl.BlockSpec((tm, D_MODEL), row), _resident((1, D_MODEL)), _resident(wqT.shape),
                _resident(wkv.shape), _resident(wvT.shape), _resident(wrest.shape), _resident(wdt.shape)]
    return pl.pallas_call(
        _in_proj_kernel, out_shape=out_shape, grid=(n // tm,), in_specs=in_specs, out_specs=out_specs,
        compiler_params=_cparams("parallel"), name="in_proj",
    )(x, norm_g.reshape(1, D_MODEL), wqT, wkv, wvT, wrest, wdt)


def _moba_prompt_kernel(qT_ref, ka_ref, vT_ref, ksum_ref, g_ref, o_ref, qa_ref, m_ref, acc_ref,
                        s0_ref, c0_ref, s1_ref, c1_ref):
    i = pl.program_id(1)
    n_blk = ksum_ref.shape[0]
    kbar = (ksum_ref[...] * (1.0 / MOBA_BLOCK)).astype(BF16)
    blk = lax.broadcasted_iota(jnp.int32, (n_blk, MOBA_BLOCK), 0)
    key_pos = lax.broadcasted_iota(jnp.int32, (MOBA_BLOCK, MOBA_BLOCK), 0)
    qry_pos = lax.broadcasted_iota(jnp.int32, (MOBA_BLOCK, MOBA_BLOCK), 1)

    for h in range(Q_PER_KV):
        cols = slice(h * MOBA_BLOCK, (h + 1) * MOBA_BLOCK)
        qT = qT_ref[h * HEAD_DIM:(h + 1) * HEAD_DIM, :]
        gate = jnp.dot(kbar, qT, preferred_element_type=F32)
        gate = jnp.where(blk < i, gate, -jnp.inf)
        sel = blk < 0
        for r in range(MOBA_TOPK):
            best = jnp.max(gate, axis=0, keepdims=True)
            first = jnp.min(jnp.where(gate == best, blk, n_blk), axis=0, keepdims=True)
            pick = jnp.logical_and(blk == first, r < i)
            sel = jnp.logical_or(sel, pick)
            gate = jnp.where(pick, -jnp.inf, gate)
        qa_ref[:HEAD_DIM, cols] = qT
        qa_ref[HEAD_DIM:HEAD_DIM + n_blk, cols] = jnp.where(sel, 0.0, MASK_BIAS).astype(BF16)
        if n_blk < LANES:
            qa_ref[HEAD_DIM + n_blk:, cols] = jnp.zeros((LANES - n_blk, MOBA_BLOCK), BF16)

    k_own = ka_ref[i][:, :HEAD_DIM]
    for h in range(Q_PER_KV):
        s = jnp.dot(k_own, qT_ref[h * HEAD_DIM:(h + 1) * HEAD_DIM, :], preferred_element_type=F32)
        s = jnp.where(key_pos <= qry_pos, s, MASK_BIAS)
        m_new = _reduce_rows(s, jnp.maximum, jnp.max)
        m_ref[h] = m_new
        acc_ref[h] = jnp.dot(vT_ref[i], jnp.exp2(s - m_new).astype(BF16), preferred_element_type=F32)

    kb = KEY_BLOCKS_PER_STEP
    n_groups = (i + kb - 1) // kb
    last_group = n_groups - 1

    def logits(group, dst_ref, dmax_ref):
        k_aug = ka_ref[pl.ds(kb * group, kb)].reshape(kb * MOBA_BLOCK, 2 * HEAD_DIM)
        for h in range(Q_PER_KV):
            s = jnp.dot(k_aug, qa_ref[:, h * MOBA_BLOCK:(h + 1) * MOBA_BLOCK],
                        preferred_element_type=F32)
            dst_ref[h] = s
            dmax_ref[h] = _reduce_rows(s, jnp.maximum, jnp.max)

    def softmax_step(group, cur_ref, cmax_ref, nxt_ref, nmax_ref):
        logits(jnp.minimum(group + 1, last_group), nxt_ref, nmax_ref)
        vT = jnp.concatenate([vT_ref[kb * group + b] for b in range(kb)], axis=1)
        for h in range(Q_PER_KV):
            m_old = m_ref[h]
            m_new = jnp.maximum(m_old, cmax_ref[h])
            p = jnp.exp2(cur_ref[h] - m_new).astype(BF16)
            acc_ref[h] = jnp.exp2(m_old - m_new) * acc_ref[h] + jnp.dot(vT, p, preferred_element_type=F32)
            m_ref[h] = m_new

    logits(0, s0_ref, c0_ref)

    def one_group(group, carry):
        @pl.when(group % 2 == 0)
        def _():
            softmax_step(group, s0_ref, c0_ref, s1_ref, c1_ref)

        @pl.when(group % 2 == 1)
        def _():
            softmax_step(group, s1_ref, c1_ref, s0_ref, c0_ref)

        return carry

    lax.fori_loop(0, n_groups, one_group, 0)

    for h in range(Q_PER_KV):
        acc = acc_ref[h]
        out = (acc[:HEAD_DIM] / acc[HEAD_DIM:HEAD_DIM + 1]).T
        dcols = slice(h * HEAD_DIM, (h + 1) * HEAD_DIM)
        o_ref[:, dcols] = (out * _silu(g_ref[:, dcols])).astype(BF16)


def _moba_prompt(qT, ka, vT, ksum, gz):
    n = qT.shape[1]
    n_blk = n // MOBA_BLOCK
    assert n_blk <= LANES and n_blk % KEY_BLOCKS_PER_STEP == 0
    kv_rows = Q_PER_KV * HEAD_DIM
    logit_buf = pltpu.VMEM((Q_PER_KV, KEY_BLOCKS_PER_STEP * MOBA_BLOCK, MOBA_BLOCK), F32)
    stat_buf = pltpu.VMEM((Q_PER_KV, 1, MOBA_BLOCK), F32)
    return pl.pallas_call(
        _moba_prompt_kernel,
        out_shape=jax.ShapeDtypeStruct((n, ATT_WIDTH), BF16),
        grid=(N_KV_HEADS, n_blk),
        in_specs=[
            pl.BlockSpec((kv_rows, MOBA_BLOCK), lambda h, i: (h, i)),
            pl.BlockSpec((None, n_blk, MOBA_BLOCK, 2 * HEAD_DIM), lambda h, i: (h, 0, 0, 0)),
            pl.BlockSpec((n_blk, VT_ROWS, MOBA_BLOCK), lambda h, i: (0, h, 0)),
            pl.BlockSpec((n_blk, HEAD_DIM), lambda h, i: (0, h)),
            pl.BlockSpec((MOBA_BLOCK, kv_rows), lambda h, i: (i, h)),
        ],
        out_specs=pl.BlockSpec((MOBA_BLOCK, kv_rows), lambda h, i: (i, h)),
        scratch_shapes=[pltpu.VMEM((2 * HEAD_DIM, Q_PER_KV * MOBA_BLOCK), BF16),
                        stat_buf,
                        pltpu.VMEM((Q_PER_KV, VT_ROWS, MOBA_BLOCK), F32),
                        logit_buf, stat_buf, logit_buf, stat_buf],
        compiler_params=_cparams("parallel", "parallel"), name="moba_prompt",
    )(qT, ka, vT, ksum, gz)


def _ssd_prompt_kernel(xbc_ref, z_ref, dt_ref, convw_ref, convb_ref, dtb_ref, acoef_ref, acoefe_ref, dskipe_ref,
                       ng_ref, o_ref, state_ref, xpad_ref, st_ref, y_ref):
    c = pl.program_id(0)
    t = SSD_CHUNK

    @pl.when(c == 0)
    def _():
        xpad_ref[:SUBLANES, :] = jnp.zeros((SUBLANES, CONV_DIM), F32)
        st_ref[...] = jnp.zeros_like(st_ref)

    xpad_ref[SUBLANES:, :] = xbc_ref[...]
    conv = convb_ref[...] + convw_ref[CONV_WIDTH - 1:CONV_WIDTH, :] * xpad_ref[SUBLANES:, :]
    for w in range(CONV_WIDTH - 1):
        back = CONV_WIDTH - 1 - w
        conv = conv + convw_ref[w:w + 1, :] * xpad_ref[SUBLANES - back:SUBLANES - back + t, :]
    xpad_ref[:SUBLANES, :] = xpad_ref[t:t + SUBLANES, :]
    xc = _silu(conv)
    xs = xc[:, :SSD_WIDTH]

    dt = jax.nn.softplus(dt_ref[...] + dtb_ref[...])
    head_of_channel = lax.broadcasted_iota(jnp.int32, (LANES, SSD_WIDTH), 1) // SSD_HEAD_DIM
    expand = (head_of_channel == lax.broadcasted_iota(jnp.int32, (LANES, SSD_WIDTH), 0)).astype(BF16)
    dt_e = _dot3(_split3(dt), expand)
    lower = lax.broadcasted_iota(jnp.int32, (t, t), 0) >= lax.broadcasted_iota(jnp.int32, (t, t), 1)
    tril = lower.astype(BF16)
    cumsum = lambda v: _dot3(_split3(v), tril, left=True)
    a_cum = cumsum(dt * acoef_ref[...])
    a_cum_t = a_cum.T
    a_cum_e = cumsum(dt_e * acoefe_ref[...])
    a_end_e = a_cum_e[t - 1:t, :]

    xdt = xs * dt_e
    xdt_end = (xdt * jnp.exp(a_end_e - a_cum_e)).astype(BF16)
    from_start = jnp.exp(a_cum_e)
    state_decay = jnp.exp(a_end_e)
    lane = lax.broadcasted_iota(jnp.int32, (t, LANES), 1)

    for g in range(SSD_GROUPS):
        gc = slice(g * GROUP_WIDTH, (g + 1) * GROUP_WIDTH)
        b_bf = xc[:, SSD_WIDTH + g * D_STATE:SSD_WIDTH + (g + 1) * D_STATE].astype(BF16)
        c_bf = xc[:, SSD_WIDTH + (SSD_GROUPS + g) * D_STATE:
                  SSD_WIDTH + (SSD_GROUPS + g + 1) * D_STATE].astype(BF16)
        cb = lax.dot_general(c_bf, b_bf, NT_DIMS, preferred_element_type=F32)
        st = st_ref[g]
        y_ref[:, gc] = jnp.dot(c_bf, st.astype(BF16), preferred_element_type=F32) * from_start[:, gc]
        st_ref[g] = st * state_decay[:, gc] + lax.dot_general(b_bf, xdt_end[:, gc], TN_DIMS,
                                                               preferred_element_type=F32)
        for pair in range(GROUP_WIDTH // LANES):
            pc = slice(g * GROUP_WIDTH + pair * LANES, g * GROUP_WIDTH + (pair + 1) * LANES)
            xdt_pair = xdt[:, pc]
            y_pair = y_ref[:, pc] + dskipe_ref[:, pc] * xs[:, pc]
            for hh in range(LANES // SSD_HEAD_DIM):
                h = (g * GROUP_WIDTH + pair * LANES) // SSD_HEAD_DIM + hh
                decay = jnp.exp(jnp.where(lower, a_cum[:, h:h + 1] - a_cum_t[h:h + 1, :], -jnp.inf))
                mine = (lane // SSD_HEAD_DIM) == hh
                rhs = jnp.where(mine, xdt_pair, 0.0).astype(BF16)
                y_pair = y_pair + jnp.dot((cb * decay).astype(BF16), rhs, preferred_element_type=F32)
            y_ref[:, pc] = y_pair

    y = y_ref[...] * _silu(z_ref[...])
    o_ref[...] = _rmsnorm(y, ng_ref[...]).astype(o_ref.dtype)

    @pl.when(c == pl.num_programs(0) - 1)
    def _():
        for g in range(SSD_GROUPS):
            for pair in range(GROUP_WIDTH // LANES):
                both = st_ref[g][:, pair * LANES:(pair + 1) * LANES].T
                for hh in range(LANES // SSD_HEAD_DIM):
                    h = (g * GROUP_WIDTH + pair * LANES) // SSD_HEAD_DIM + hh
                    state_ref[h] = both[hh * SSD_HEAD_DIM:(hh + 1) * SSD_HEAD_DIM, :]


def _ssd_prompt(xbc, gz, dt_raw, conv_w, conv_b, dt_bias, a_log, d_skip, ssd_norm_g):
    n = xbc.shape[0]
    t = SSD_CHUNK
    assert n % t == 0
    pad = lambda v: jnp.pad(v.reshape(1, N_SSD_HEADS), ((0, 0), (0, LANES - N_SSD_HEADS)))
    per_channel = lambda v: jnp.repeat(v.astype(F32), SSD_HEAD_DIM).reshape(1, SSD_WIDTH)
    a_coef = -jnp.exp(a_log.astype(F32))
    row = lambda c: (c, 0)
    return pl.pallas_call(
        _ssd_prompt_kernel,
        out_shape=(jax.ShapeDtypeStruct((n, SSD_WIDTH), BF16),
                   jax.ShapeDtypeStruct((N_SSD_HEADS, SSD_HEAD_DIM, D_STATE), F32)),
        grid=(n // t,),
        in_specs=[pl.BlockSpec((t, CONV_DIM), row),
                  pl.BlockSpec((t, SSD_WIDTH), lambda c: (c, 1)),
                  pl.BlockSpec((t, LANES), row),
                  _resident((CONV_WIDTH, CONV_DIM)), _resident((1, CONV_DIM)), _resident((1, LANES)),
                  _resident((1, LANES)), _resident((1, SSD_WIDTH)), _resident((1, SSD_WIDTH)),
                  _resident((1, SSD_WIDTH))],
        out_specs=(pl.BlockSpec((t, SSD_WIDTH), row),
                   pl.BlockSpec((N_SSD_HEADS, SSD_HEAD_DIM, D_STATE), lambda c: (0, 0, 0))),
        scratch_shapes=[pltpu.VMEM((t + SUBLANES, CONV_DIM), F32),
                        pltpu.VMEM((SSD_GROUPS, D_STATE, GROUP_WIDTH), F32),
                        pltpu.VMEM((t, SSD_WIDTH), F32)],
        compiler_params=_cparams("arbitrary"), name="ssd_prompt",
    )(xbc, gz, dt_raw, conv_w, conv_b.reshape(1, CONV_DIM), pad(dt_bias), pad(a_coef), per_channel(a_coef),
      per_channel(d_skip), ssd_norm_g.reshape(1, SSD_WIDTH))


def _out_proj_kernel(x_ref, oa_ref, os_ref, w_ref, fg_ref, h_ref, *, final_norm):
    out = jnp.dot(oa_ref[...], w_ref[:ATT_WIDTH, :], preferred_element_type=F32)
    out = out + jnp.dot(os_ref[...], w_ref[ATT_WIDTH:, :], preferred_element_type=F32)
    h = x_ref[...] + out
    h_ref[...] = _rmsnorm(h, fg_ref[...]) if final_norm else h


def _out_proj(x, o_att, o_ssd, w_out_bf, final_g, final_norm, tm):
    n = x.shape[0]
    assert n % tm == 0
    row = lambda i: (i, 0)
    return pl.pallas_call(
        functools.partial(_out_proj_kernel, final_norm=final_norm),
        out_shape=jax.ShapeDtypeStruct((n, D_MODEL), F32),
        grid=(n // tm,),
        in_specs=[pl.BlockSpec((tm, D_MODEL), row), pl.BlockSpec((tm, ATT_WIDTH), row),
                  pl.BlockSpec((tm, SSD_WIDTH), row), _resident((MIX_WIDTH, D_MODEL)), _resident((1, D_MODEL))],
        out_specs=pl.BlockSpec((tm, D_MODEL), row),
        compiler_params=_cparams("parallel"), name="out_proj",
    )(x, o_att, o_ssd, w_out_bf, final_g.reshape(1, D_MODEL))


def _prompt_layer(x, norm_g, w_in, conv_w, conv_b, dt_bias, a_log, d_skip, ssd_norm_g, w_out, final_g,
                  final_norm, tm):
    n = x.shape[0]
    qT, k, v, ka, vT, gz, xbc, dt_raw, ksum = _in_proj(x, norm_g, _in_proj_weights(w_in), tm)
    o_att = _moba_prompt(qT, ka, vT, ksum.reshape(n // MOBA_BLOCK, KV_WIDTH), gz)
    o_ssd, state = _ssd_prompt(xbc, gz, dt_raw, conv_w, conv_b, dt_bias, a_log, d_skip, ssd_norm_g)
    h = _out_proj(x, o_att, o_ssd, w_out.astype(BF16), final_g, final_norm, tm)
    return h, k, v, xbc[n - (CONV_WIDTH - 1):], state


PAGES_PER_BLOCK = MOBA_BLOCK // PAGE_SIZE
PAGE_ROWS = PAGE_SIZE * N_KV_HEADS
SAMPLE_U_WIDTH = 49 * LANES


def _in_proj_sample_kernel(x_ref, g_ref, w_ref, u_ref):
    xn = _rmsnorm(x_ref[...], g_ref[...]).astype(BF16)
    u_ref[...] = jnp.dot(xn, w_ref[...], preferred_element_type=F32)


def _in_proj_sample(x, norm_g, w_in):
    n = x.shape[0]
    wb = w_in.astype(BF16)
    w = jnp.concatenate([wb, jnp.repeat(wb[:, XBC_END:], SSD_HEAD_DIM, axis=1)], axis=1)
    w = jnp.pad(w, ((0, 0), (0, SAMPLE_U_WIDTH - w.shape[1])))
    return pl.pallas_call(
        _in_proj_sample_kernel, out_shape=jax.ShapeDtypeStruct((n, SAMPLE_U_WIDTH), F32), grid=(1,),
        in_specs=[_resident((n, D_MODEL)), _resident((1, D_MODEL)), _resident(w.shape)],
        out_specs=pl.BlockSpec((n, SAMPLE_U_WIDTH), lambda i: (0, 0)),
        compiler_params=_cparams("arbitrary"), name="in_proj_sample",
    )(x, norm_g.reshape(1, D_MODEL), w)


def _kbar_sample_kernel(pt_ref, cache_ref, out_ref, buf_ref, sem_ref, *, layer):
    b = pl.program_id(0)
    n_pages = pt_ref.shape[1]
    slot = b % 2

    def copies(seq, slot):
        return [pltpu.make_async_copy(cache_ref.at[layer, pt_ref[seq, p]], buf_ref.at[slot, p], sem_ref.at[slot])
                for p in range(n_pages)]

    @pl.when(b == 0)
    def _():
        for cp in copies(0, 0):
            cp.start()

    @pl.when(b + 1 < pl.num_programs(0))
    def _():
        for cp in copies(b + 1, 1 - slot):
            cp.start()

    for cp in copies(b, slot):
        cp.wait()
    for blk in range(n_pages // PAGES_PER_BLOCK):
        acc = jnp.zeros((SUBLANES, HEAD_DIM), F32)
        for p in range(PAGES_PER_BLOCK):
            page = buf_ref[slot, blk * PAGES_PER_BLOCK + p]
            acc = acc + jnp.sum(page.reshape(PAGE_ROWS // SUBLANES, SUBLANES, HEAD_DIM), axis=0)
        acc = acc + pltpu.roll(acc, 2, 0)
        acc = acc + pltpu.roll(acc, 4, 0)
        for h in range(N_KV_HEADS):
            out_ref[0, h, blk:blk + 1, :] = acc[h:h + 1, :] * (1.0 / MOBA_BLOCK)


def _kbar_sample(cache_pages, page_table, layer):
    n_seq, n_pages = page_table.shape
    assert n_pages % PAGES_PER_BLOCK == 0
    n_blk = n_pages // PAGES_PER_BLOCK
    return pl.pallas_call(
        functools.partial(_kbar_sample_kernel, layer=layer),
        out_shape=jax.ShapeDtypeStruct((n_seq, N_KV_HEADS, n_blk, HEAD_DIM), F32),
        grid_spec=pltpu.PrefetchScalarGridSpec(
            num_scalar_prefetch=1, grid=(n_seq,),
            in_specs=[pl.BlockSpec(memory_space=pl.ANY)],
            out_specs=pl.BlockSpec((1, N_KV_HEADS, n_blk, HEAD_DIM), lambda b, pt: (b, 0, 0, 0)),
            scratch_shapes=[pltpu.VMEM((2, n_pages, PAGE_ROWS, HEAD_DIM), F32),
                            pltpu.SemaphoreType.DMA((2,))]),
        compiler_params=_cparams("arbitrary"), name="kbar_sample",
    )(page_table, cache_pages)


def _gate_sample_kernel(q_ref, kbar_ref, idx_ref, gate_ref):
    n_seq, _, n_blk, _ = kbar_ref.shape
    for b in range(n_seq):
        for kh in range(N_KV_HEADS):
            r0 = b * N_ATT_HEADS + kh * Q_PER_KV
            q4 = q_ref[r0:r0 + Q_PER_KV, :].astype(BF16)
            gate_ref[r0:r0 + Q_PER_KV, :] = lax.dot_general(
                q4, kbar_ref[b, kh].astype(BF16), NT_DIMS, preferred_element_type=F32)
    gate = gate_ref[...]
    lane = lax.broadcasted_iota(jnp.int32, gate.shape, 1)
    idx_ref[...] = jnp.zeros_like(idx_ref)
    for r in range(MOBA_TOPK):
        best = jnp.max(gate, axis=1, keepdims=True)
        first = jnp.min(jnp.where(gate == best, lane, n_blk), axis=1, keepdims=True)
        idx_ref[:, r:r + 1] = first
        gate = jnp.where(lane == first, -jnp.inf, gate)


def _gate_sample(q_rows, kbar):
    n_seq, _, n_blk, _ = kbar.shape
    assert n_blk >= MOBA_TOPK
    rows = n_seq * N_ATT_HEADS
    return pl.pallas_call(
        _gate_sample_kernel, out_shape=jax.ShapeDtypeStruct((rows, LANES), jnp.int32), grid=(1,),
        in_specs=[_resident((rows, HEAD_DIM)), _resident(kbar.shape)],
        out_specs=pl.BlockSpec((rows, LANES), lambda i: (0, 0)),
        scratch_shapes=[pltpu.VMEM((rows, n_blk), F32)],
        compiler_params=_cparams("arbitrary"), name="gate_sample",
    )(q_rows, kbar)


def _attn_sample_kernel(pt_ref, idx_ref, q_ref, knew_ref, vnew_ref, g_ref, ck_ref, cv_ref, o_ref,
                        kbuf_ref, vbuf_ref, sem_ref, *, layer):
    b = pl.program_id(0)
    n_sel = N_ATT_HEADS * MOBA_TOPK
    cur = b % 2

    def copies(seq, buf):
        out = []
        for s in range(n_sel):
            blk = idx_ref[seq * n_sel + s]
            for p in range(PAGES_PER_BLOCK):
                page = pt_ref[seq, blk * PAGES_PER_BLOCK + p]
                slot = s * PAGES_PER_BLOCK + p
                out.append(pltpu.make_async_copy(ck_ref.at[layer, page], kbuf_ref.at[buf, slot], sem_ref.at[buf, 0]))
                out.append(pltpu.make_async_copy(cv_ref.at[layer, page], vbuf_ref.at[buf, slot], sem_ref.at[buf, 1]))
        return out

    @pl.when(b == 0)
    def _():
        for cp in copies(0, 0):
            cp.start()

    @pl.when(b + 1 < pl.num_programs(0))
    def _():
        for cp in copies(b + 1, 1 - cur):
            cp.start()

    for cp in copies(b, cur):
        cp.wait()

    row_head = lax.broadcasted_iota(jnp.int32, (1, PAGE_ROWS), 1) % N_KV_HEADS
    for h in range(N_ATT_HEADS):
        kh = h // Q_PER_KV
        qh = q_ref[0, h:h + 1, :].astype(BF16)
        q8 = jnp.broadcast_to(qh, (SUBLANES, HEAD_DIM))
        valid = row_head == kh
        scores = []
        for s in range(MOBA_TOPK * PAGES_PER_BLOCK):
            page = kbuf_ref[cur, h * MOBA_TOPK * PAGES_PER_BLOCK + s].astype(BF16)
            sc = lax.dot_general(q8, page, NT_DIMS, preferred_element_type=F32)[0:1, :]
            scores.append(jnp.where(valid, sc, -jnp.inf))
        k_self = knew_ref[0, kh:kh + 1, :].astype(BF16).astype(F32)
        s_self = jnp.sum(qh.astype(F32) * k_self, axis=1, keepdims=True)
        m = s_self
        for sc in scores:
            m = jnp.maximum(m, jnp.max(sc, axis=1, keepdims=True))
        p_self = jnp.exp((s_self - m) * ATT_SCALE)
        l = p_self
        acc = p_self.astype(BF16).astype(F32) * vnew_ref[0, kh:kh + 1, :].astype(BF16).astype(F32)
        for s, sc in enumerate(scores):
            p = jnp.exp((sc - m) * ATT_SCALE)
            l = l + jnp.sum(p, axis=1, keepdims=True)
            p8 = jnp.broadcast_to(p.astype(BF16), (SUBLANES, PAGE_ROWS))
            page = vbuf_ref[cur, h * MOBA_TOPK * PAGES_PER_BLOCK + s].astype(BF16)
            acc = acc + jnp.dot(p8, page, preferred_element_type=F32)[0:1, :]
        o_ref[0, h:h + 1, :] = ((acc / l) * _silu(g_ref[0, h:h + 1, :])).astype(o_ref.dtype)


def _attn_sample(q3, knew3, vnew3, g3, cache_k_pages, cache_v_pages, page_table, idx_flat, layer):
    n_seq = q3.shape[0]
    n_slots = N_ATT_HEADS * MOBA_TOPK * PAGES_PER_BLOCK
    per_seq = lambda r: pl.BlockSpec((1, r, HEAD_DIM), lambda b, pt, ix: (b, 0, 0))
    return pl.pallas_call(
        functools.partial(_attn_sample_kernel, layer=layer),
        out_shape=jax.ShapeDtypeStruct((n_seq, N_ATT_HEADS, HEAD_DIM), BF16),
        grid_spec=pltpu.PrefetchScalarGridSpec(
            num_scalar_prefetch=2, grid=(n_seq,),
            in_specs=[per_seq(N_ATT_HEADS), per_seq(N_KV_HEADS), per_seq(N_KV_HEADS), per_seq(N_ATT_HEADS),
                      pl.BlockSpec(memory_space=pl.ANY), pl.BlockSpec(memory_space=pl.ANY)],
            out_specs=per_seq(N_ATT_HEADS),
            scratch_shapes=[pltpu.VMEM((2, n_slots, PAGE_ROWS, HEAD_DIM), F32),
                            pltpu.VMEM((2, n_slots, PAGE_ROWS, HEAD_DIM), F32),
                            pltpu.SemaphoreType.DMA((2, 2))]),
        compiler_params=_cparams("arbitrary"), name="attn_sample",
    )(page_table, idx_flat, q3, knew3, vnew3, g3, cache_k_pages, cache_v_pages)


def _column(row):
    return jnp.broadcast_to(row, (LANES, LANES)).T


def _ssd_sample_kernel(xbc_ref, z_ref, dt_ref, sc_ref, ss_ref, convw_ref, convb_ref, dtb_ref, acoef_ref,
                       dskip_ref, ng_ref, o_ref, nc_ref, ns_ref, y_ref):
    hist = sc_ref[0]
    new = xbc_ref[0]
    conv = convb_ref[...] + convw_ref[CONV_WIDTH - 1:CONV_WIDTH, :] * new
    for w in range(CONV_WIDTH - 1):
        conv = conv + convw_ref[w:w + 1, :] * hist[w:w + 1, :]
    nc_ref[0, :CONV_WIDTH - 2, :] = hist[1:, :]
    nc_ref[0, CONV_WIDTH - 2:, :] = new
    xc = _silu(conv)
    xs = xc[:, :SSD_WIDTH]
    dt = jax.nn.softplus(dt_ref[0] + dtb_ref[...])
    d_a = jnp.exp(dt * acoef_ref[...])
    xdt = xs * dt
    heads_per_chunk = LANES // SSD_HEAD_DIM
    for c in range(SSD_WIDTH // LANES):
        lanes = slice(c * LANES, (c + 1) * LANES)
        x_col = _column(xdt[:, lanes])
        da_col = _column(d_a[:, lanes])
        y_cols = []
        for hh in range(heads_per_chunk):
            h = c * heads_per_chunk + hh
            g = h // HEADS_PER_GROUP
            rows = slice(hh * SSD_HEAD_DIM, (hh + 1) * SSD_HEAD_DIM)
            b_g = xc[:, SSD_WIDTH + g * D_STATE:SSD_WIDTH + (g + 1) * D_STATE]
            c_g = xc[:, SSD_WIDTH + (SSD_GROUPS + g) * D_STATE:SSD_WIDTH + (SSD_GROUPS + g + 1) * D_STATE]
            st = ss_ref[0, h] * da_col[rows, :] + x_col[rows, :] * b_g
            ns_ref[0, h] = st
            y_cols.append(jnp.sum(st * c_g, axis=1, keepdims=True))
        y_col = jnp.concatenate(y_cols, axis=0)
        y_ref[:, lanes] = jnp.broadcast_to(y_col, (LANES, LANES)).T[0:1, :]
    y = (y_ref[...] + dskip_ref[...] * xs) * _silu(z_ref[0])
    o_ref[0] = _rmsnorm(y, ng_ref[...]).astype(o_ref.dtype)


def _ssd_sample(xbc3, z3, dte3, state_conv, state_ssm, layer, conv_w, conv_b, dt_bias, a_log, d_skip, ssd_norm_g):
    n_seq = xbc3.shape[0]
    per_channel = lambda v: jnp.repeat(v.astype(F32), SSD_HEAD_DIM).reshape(1, SSD_WIDTH)
    seq3 = lambda w: pl.BlockSpec((1, 1, w), lambda b: (b, 0, 0))
    state_spec = pl.BlockSpec((None, 1, N_SSD_HEADS, SSD_HEAD_DIM, D_STATE), lambda b: (layer, b, 0, 0, 0))
    in_specs = [seq3(CONV_DIM), seq3(SSD_WIDTH), seq3(SSD_WIDTH),
                pl.BlockSpec((None, 1, CONV_WIDTH - 1, CONV_DIM), lambda b: (layer, b, 0, 0)), state_spec,
                _resident((CONV_WIDTH, CONV_DIM)), _resident((1, CONV_DIM)), _resident((1, SSD_WIDTH)),
                _resident((1, SSD_WIDTH)), _resident((1, SSD_WIDTH)), _resident((1, SSD_WIDTH))]
    operands = [xbc3, z3, dte3, state_conv, state_ssm, conv_w, conv_b.reshape(1, CONV_DIM), per_channel(dt_bias),
                per_channel(-jnp.exp(a_log.astype(F32))), per_channel(d_skip), ssd_norm_g.reshape(1, SSD_WIDTH)]
    return pl.pallas_call(
        _ssd_sample_kernel,
        out_shape=(jax.ShapeDtypeStruct((n_seq, 1, SSD_WIDTH), BF16),
                   jax.ShapeDtypeStruct((n_seq, CONV_WIDTH - 1, CONV_DIM), F32),
                   jax.ShapeDtypeStruct((n_seq, N_SSD_HEADS, SSD_HEAD_DIM, D_STATE), F32)),
        grid=(n_seq,), in_specs=in_specs,
        out_specs=(seq3(SSD_WIDTH), pl.BlockSpec((1, CONV_WIDTH - 1, CONV_DIM), lambda b: (b, 0, 0)),
                   pl.BlockSpec((1, N_SSD_HEADS, SSD_HEAD_DIM, D_STATE), lambda b: (b, 0, 0, 0))),
        scratch_shapes=[pltpu.VMEM((1, SSD_WIDTH), F32)],
        compiler_params=_cparams("parallel"), name="ssd_sample",
    )(*operands)


def _sample_layer(x, layer, cache_k_pages, cache_v_pages, state_conv, state_ssm, page_table,
                  norm_g, w_in, conv_w, conv_b, dt_bias, a_log, d_skip, ssd_norm_g, w_out, final_g, final_norm):
    n_seq = x.shape[0]
    u = _in_proj_sample(x, norm_g, w_in)
    q, k, v, g_att = u[:, :Q_END], u[:, Q_END:K_END], u[:, K_END:V_END], u[:, V_END:G_END]
    z, xbc, dte = u[:, G_END:Z_END], u[:, Z_END:XBC_END], u[:, IN_WIDTH:IN_WIDTH + SSD_WIDTH]
    kbar = _kbar_sample(cache_k_pages, page_table, layer)
    idx = _gate_sample(q.reshape(n_seq * N_ATT_HEADS, HEAD_DIM), kbar)
    heads = lambda a, r: a.reshape(n_seq, r, HEAD_DIM)
    o_att = _attn_sample(heads(q, N_ATT_HEADS), heads(k, N_KV_HEADS), heads(v, N_KV_HEADS),
                         heads(g_att, N_ATT_HEADS), cache_k_pages, cache_v_pages, page_table,
                         idx[:, :MOBA_TOPK].reshape(-1), layer)
    rows3 = lambda a: a.reshape(n_seq, 1, a.shape[1])
    o_ssd, new_conv, new_ssm = _ssd_sample(rows3(xbc), rows3(z), rows3(dte), state_conv, state_ssm, layer,
                                           conv_w, conv_b, dt_bias, a_log, d_skip, ssd_norm_g)
    h = _out_proj(x, o_att.reshape(n_seq, ATT_WIDTH), o_ssd.reshape(n_seq, SSD_WIDTH), w_out.astype(BF16),
                  final_g, final_norm, n_seq)
    return h, k, v, new_conv, new_ssm


PROMPT_ROW_TILE = 512


def kernel(x_prompt, x_sample, cache_k, cache_v, state_conv, state_ssm, page_table, norm_g, w_in, conv_w,
           conv_b, dt_bias, a_log, d_skip, ssd_norm_g, w_out, final_norm_g):
    depth = w_in.shape[0]
    batch, seq, _ = x_prompt.shape
    n_seq, dec_seq, _ = x_sample.shape
    assert batch == 1 and dec_seq == 1
    n_phys = cache_k.shape[1]
    ck = cache_k.reshape(depth, n_phys, PAGE_ROWS, HEAD_DIM)
    cv = cache_v.reshape(depth, n_phys, PAGE_ROWS, HEAD_DIM)
    hp, hs = x_prompt[0], x_sample[:, 0]
    outs = [[] for _ in range(8)]
    for l in range(depth):
        last = l == depth - 1
        params = (norm_g[l], w_in[l], conv_w[l], conv_b[l], dt_bias[l], a_log[l], d_skip[l], ssd_norm_g[l],
                  w_out[l], final_norm_g, last)
        hp, kp, vp, cp, sp = _prompt_layer(hp, *params, PROMPT_ROW_TILE)
        hs, ks, vs, cs, ss = _sample_layer(hs, l, ck, cv, state_conv, state_ssm, page_table, *params)
        kv4 = lambda a: a.reshape(n_seq, dec_seq, N_KV_HEADS, HEAD_DIM)
        for lst, val in zip(outs, (kp[None], vp[None], cp[None], sp[None], kv4(ks), kv4(vs), cs, ss)):
            lst.append(val)
    return (hp[None], hs[:, None]) + tuple(jnp.stack(o) for o in outs)
```

```python
import functools

import jax
import jax.numpy as jnp
from jax import lax
from jax.experimental import pallas as pl
from jax.experimental.pallas import tpu as pltpu

F32 = jnp.float32
BF16 = jnp.bfloat16

D_MODEL = 1024
N_ATT_HEADS = 8
N_KV_HEADS = 2
HEAD_DIM = 128
Q_PER_KV = N_ATT_HEADS // N_KV_HEADS
ATT_WIDTH = N_ATT_HEADS * HEAD_DIM
KV_WIDTH = N_KV_HEADS * HEAD_DIM
MOBA_BLOCK = 256
MOBA_TOPK = 3
PAGE_SIZE = 128
SSD_WIDTH = 1024
SSD_HEAD_DIM = 64
N_SSD_HEADS = SSD_WIDTH // SSD_HEAD_DIM
SSD_GROUPS = 2
HEADS_PER_GROUP = N_SSD_HEADS // SSD_GROUPS
GROUP_WIDTH = HEADS_PER_GROUP * SSD_HEAD_DIM
D_STATE = 128
CONV_WIDTH = 4
SSD_CHUNK = 256
CONV_DIM = SSD_WIDTH + 2 * SSD_GROUPS * D_STATE
MIX_WIDTH = ATT_WIDTH + SSD_WIDTH
Q_END = ATT_WIDTH
K_END = Q_END + KV_WIDTH
V_END = K_END + KV_WIDTH
G_END = V_END + ATT_WIDTH
Z_END = G_END + SSD_WIDTH
XBC_END = Z_END + CONV_DIM
IN_WIDTH = XBC_END + N_SSD_HEADS
NORM_EPS = 1e-5
ATT_SCALE = HEAD_DIM ** -0.5
EXP2_SCALE = ATT_SCALE * 1.4426950408889634

LANES = 128
SUBLANES = 8
VMEM_LIMIT_BYTES = 56 * 1024 * 1024

MASK_BIAS = -1e30
VT_ONES = 16
VT_ROWS = HEAD_DIM + VT_ONES
KEY_BLOCKS_PER_STEP = 2

NT_DIMS = (((1,), (1,)), ((), ()))
TN_DIMS = (((0,), (0,)), ((), ()))


def _cparams(*semantics):
    return pltpu.CompilerParams(dimension_semantics=semantics, vmem_limit_bytes=VMEM_LIMIT_BYTES)


def _resident(shape):
    nd = len(shape)
    return pl.BlockSpec(shape, lambda *_: (0,) * nd, pipeline_mode=pl.Buffered(1))


def _rmsnorm(x, g):
    return x * lax.rsqrt(jnp.mean(x * x, axis=-1, keepdims=True) + NORM_EPS) * g


def _silu(x):
    return x * jax.nn.sigmoid(x)


def _reduce_rows(x, op, final):
    slabs = [x[r:r + SUBLANES] for r in range(0, x.shape[0], SUBLANES)]
    while len(slabs) > 1:
        nxt = [op(slabs[a], slabs[a + 1]) for a in range(0, len(slabs) - 1, 2)]
        if len(slabs) % 2:
            nxt.append(slabs[-1])
        slabs = nxt
    return final(slabs[0], axis=0, keepdims=True)


def _split3(x):
    hi = x.astype(BF16)
    r = x - hi.astype(F32)
    mid = r.astype(BF16)
    lo = (r - mid.astype(F32)).astype(BF16)
    return hi, mid, lo


def _dot3(parts, m, left=False):
    if left:
        f = lambda p: jnp.dot(m, p, preferred_element_type=F32)
    else:
        f = lambda p: jnp.dot(p, m, preferred_element_type=F32)
    return (f(parts[2]) + f(parts[1])) + f(parts[0])


def _in_proj_kernel(x_ref, g_ref, wqT_ref, wkv_ref, wvT_ref, wrest_ref, wdt_ref,
                    qT_ref, k_ref, v_ref, ka_ref, vT_ref, gz_ref, xbc_ref, dt_ref, ksum_ref):
    tm = x_ref.shape[0]
    xn = _rmsnorm(x_ref[...], g_ref[...]).astype(BF16)
    qT = lax.dot_general(wqT_ref[...], xn, NT_DIMS, preferred_element_type=F32)
    qT_ref[...] = (qT * EXP2_SCALE).astype(BF16)
    vT = lax.dot_general(wvT_ref[...], xn, NT_DIMS, preferred_element_type=F32).astype(BF16)
    kv = jnp.dot(xn, wkv_ref[...], preferred_element_type=F32)
    k = kv[:, :KV_WIDTH]
    for h in range(N_KV_HEADS):
        k_ref[:, h, :] = kv[:, h * HEAD_DIM:(h + 1) * HEAD_DIM]
        v_ref[:, h, :] = kv[:, KV_WIDTH + h * HEAD_DIM:KV_WIDTH + (h + 1) * HEAD_DIM]
    blk0 = pl.program_id(0) * (tm // MOBA_BLOCK)
    lane = lax.broadcasted_iota(jnp.int32, (MOBA_BLOCK, LANES), 1)
    for b in range(tm // MOBA_BLOCK):
        rows = slice(b * MOBA_BLOCK, (b + 1) * MOBA_BLOCK)
        onehot = (lane == blk0 + b).astype(BF16)
        for h in range(N_KV_HEADS):
            ka_ref[h, b, :, :HEAD_DIM] = k[rows, h * HEAD_DIM:(h + 1) * HEAD_DIM].astype(BF16)
            ka_ref[h, b, :, HEAD_DIM:] = onehot
        for h in range(N_KV_HEADS):
            vT_ref[b, h * VT_ROWS:h * VT_ROWS + HEAD_DIM, :] = vT[h * HEAD_DIM:(h + 1) * HEAD_DIM, rows]
            vT_ref[b, h * VT_ROWS + HEAD_DIM:(h + 1) * VT_ROWS, :] = jnp.ones((VT_ONES, MOBA_BLOCK), BF16)
        ksum_ref[0, b:b + 1, :] = jnp.sum(k[rows], axis=0, keepdims=True)
    rest = jnp.dot(xn, wrest_ref[...], preferred_element_type=F32)
    gz_ref[...] = rest[:, :ATT_WIDTH + SSD_WIDTH]
    xbc_ref[...] = rest[:, ATT_WIDTH + SSD_WIDTH:]
    dt_ref[...] = jnp.dot(xn, wdt_ref[...], preferred_element_type=F32)


def _in_proj_weights(w_in):
    wb = w_in.astype(BF16)
    wqT = wb[:, :Q_END].T
    wkv = wb[:, Q_END:V_END]
    wvT = wb[:, K_END:V_END].T
    wrest = wb[:, V_END:XBC_END]
    wdt = jnp.pad(wb[:, XBC_END:], ((0, 0), (0, LANES - N_SSD_HEADS)))
    return wqT, wkv, wvT, wrest, wdt


def _in_proj(x, norm_g, weights, tm):
    n = x.shape[0]
    assert n % tm == 0 and tm % MOBA_BLOCK == 0
    wqT, wkv, wvT, wrest, wdt = weights
    nb = tm // MOBA_BLOCK
    row = lambda i: (i, 0)
    kv_spec = pl.BlockSpec((tm, N_KV_HEADS, HEAD_DIM), lambda i: (i, 0, 0))
    out_shape = (
        jax.ShapeDtypeStruct((ATT_WIDTH, n), BF16),
        jax.ShapeDtypeStruct((n, N_KV_HEADS, HEAD_DIM), F32),
        jax.ShapeDtypeStruct((n, N_KV_HEADS, HEAD_DIM), F32),
        jax.ShapeDtypeStruct((N_KV_HEADS, n // MOBA_BLOCK, MOBA_BLOCK, 2 * HEAD_DIM), BF16),
        jax.ShapeDtypeStruct((n // MOBA_BLOCK, N_KV_HEADS * VT_ROWS, MOBA_BLOCK), BF16),
        jax.ShapeDtypeStruct((n, ATT_WIDTH + SSD_WIDTH), F32),
        jax.ShapeDtypeStruct((n, CONV_DIM), F32),
        jax.ShapeDtypeStruct((n, LANES), F32),
        jax.ShapeDtypeStruct((n // tm, nb, KV_WIDTH), F32),
    )
    out_specs = (
        pl.BlockSpec((ATT_WIDTH, tm), lambda i: (0, i)),
        kv_spec,
        kv_spec,
        pl.BlockSpec((N_KV_HEADS, nb, MOBA_BLOCK, 2 * HEAD_DIM), lambda i: (0, i, 0, 0)),
        pl.BlockSpec((nb, N_KV_HEADS * VT_ROWS, MOBA_BLOCK), lambda i: (i, 0, 0)),
        pl.BlockSpec((tm, ATT_WIDTH + SSD_WIDTH), row),
        pl.BlockSpec((tm, CONV_DIM), row),
        pl.BlockSpec((tm, LANES), row),
        pl.BlockSpec((1, nb, KV_WIDTH), lambda i: (i, 0, 0)),
    )
    in_specs = [pl.BlockSpec((tm, D_MODEL), row), _resident((1, D_MODEL)), _resident(wqT.shape),
                _resident(wkv.shape), _resident(wvT.shape), _resident(wrest.shape), _resident(wdt.shape)]
    return pl.pallas_call(
        _in_proj_kernel, out_shape=out_shape, grid=(n // tm,), in_specs=in_specs, out_specs=out_specs,
        compiler_params=_cparams("parallel"), name="in_proj",
    )(x, norm_g.reshape(1, D_MODEL), wqT, wkv, wvT, wrest, wdt)


def _moba_prompt_kernel(qT_ref, ka_ref, vT_ref, ksum_ref, g_ref, o_ref, qa_ref, m_ref, acc_ref,
                        s0_ref, c0_ref, s1_ref, c1_ref):
    i = pl.program_id(1)
    n_blk = ksum_ref.shape[0]
    kbar = (ksum_ref[...] * (1.0 / MOBA_BLOCK)).astype(BF16)
    blk = lax.broadcasted_iota(jnp.int32, (n_blk, MOBA_BLOCK), 0)
    key_pos = lax.broadcasted_iota(jnp.int32, (MOBA_BLOCK, MOBA_BLOCK), 0)
    qry_pos = lax.broadcasted_iota(jnp.int32, (MOBA_BLOCK, MOBA_BLOCK), 1)

    for h in range(Q_PER_KV):
        cols = slice(h * MOBA_BLOCK, (h + 1) * MOBA_BLOCK)
        qT = qT_ref[h * HEAD_DIM:(h + 1) * HEAD_DIM, :]
        gate = jnp.dot(kbar, qT, preferred_element_type=F32)
        gate = jnp.where(blk < i, gate, -jnp.inf)
        bias = jnp.full((n_blk, MOBA_BLOCK), MASK_BIAS, F32)
        for r in range(MOBA_TOPK):
            best = jnp.max(gate, axis=0, keepdims=True)
            first = jnp.min(jnp.where(gate == best, blk, n_blk), axis=0, keepdims=True)
            first = jnp.where(r < i, first, -1)
            pick = blk == first
            bias = jnp.where(pick, 0.0, bias)
            gate = jnp.where(pick, -jnp.inf, gate)
        qa_ref[:HEAD_DIM, cols] = qT
        qa_ref[HEAD_DIM:HEAD_DIM + n_blk, cols] = bias.astype(BF16)
        if n_blk < LANES:
            qa_ref[HEAD_DIM + n_blk:, cols] = jnp.zeros((LANES - n_blk, MOBA_BLOCK), BF16)

    def own_block():
        k_own = ka_ref[i][:, :HEAD_DIM]
        for h in range(Q_PER_KV):
            s = jnp.dot(k_own, qT_ref[h * HEAD_DIM:(h + 1) * HEAD_DIM, :], preferred_element_type=F32)
            s = jnp.where(key_pos <= qry_pos, s, MASK_BIAS)
            m_new = _reduce_rows(s, jnp.maximum, jnp.max)
            m_ref[h] = m_new
            acc_ref[h] = jnp.dot(vT_ref[i], jnp.exp2(s - m_new).astype(BF16), preferred_element_type=F32)

    kb = KEY_BLOCKS_PER_STEP
    n_groups = (i + kb - 1) // kb
    last_group = n_groups - 1

    def logits(group, dst_ref, dmax_ref):
        k_aug = ka_ref[pl.ds(kb * group, kb)].reshape(kb * MOBA_BLOCK, 2 * HEAD_DIM)
        for h in range(Q_PER_KV):
            s = jnp.dot(k_aug, qa_ref[:, h * MOBA_BLOCK:(h + 1) * MOBA_BLOCK],
                        preferred_element_type=F32)
            dst_ref[h] = s
            dmax_ref[h] = _reduce_rows(s, jnp.maximum, jnp.max)

    def softmax_step(group, cur_ref, cmax_ref, nxt_ref, nmax_ref):
        logits(jnp.minimum(group + 1, last_group), nxt_ref, nmax_ref)
        vT = jnp.concatenate([vT_ref[kb * group + b] for b in range(kb)], axis=1)
        for h in range(Q_PER_KV):
            m_old = m_ref[h]
            m_new = jnp.maximum(m_old, cmax_ref[h])
            p = jnp.exp2(cur_ref[h] - m_new).astype(BF16)
            acc_ref[h] = jnp.exp2(m_old - m_new) * acc_ref[h] + jnp.dot(vT, p, preferred_element_type=F32)
            m_ref[h] = m_new

    own_block()
    logits(0, s0_ref, c0_ref)

    def one_group(group, carry):
        @pl.when(group % 2 == 0)
        def _():
            softmax_step(group, s0_ref, c0_ref, s1_ref, c1_ref)

        @pl.when(group % 2 == 1)
        def _():
            softmax_step(group, s1_ref, c1_ref, s0_ref, c0_ref)

        return carry

    lax.fori_loop(0, n_groups, one_group, 0)

    for h in range(Q_PER_KV):
        acc = acc_ref[h]
        out = (acc[:HEAD_DIM] / acc[HEAD_DIM:HEAD_DIM + 1]).T
        dcols = slice(h * HEAD_DIM, (h + 1) * HEAD_DIM)
        o_ref[:, dcols] = (out * _silu(g_ref[:, dcols])).astype(BF16)


def _moba_prompt(qT, ka, vT, ksum, gz):
    n = qT.shape[1]
    n_blk = n // MOBA_BLOCK
    assert n_blk <= LANES and n_blk % KEY_BLOCKS_PER_STEP == 0
    kv_rows = Q_PER_KV * HEAD_DIM
    logit_buf = pltpu.VMEM((Q_PER_KV, KEY_BLOCKS_PER_STEP * MOBA_BLOCK, MOBA_BLOCK), F32)
    stat_buf = pltpu.VMEM((Q_PER_KV, 1, MOBA_BLOCK), F32)
    return pl.pallas_call(
        _moba_prompt_kernel,
        out_shape=jax.ShapeDtypeStruct((n, ATT_WIDTH), BF16),
        grid=(N_KV_HEADS, n_blk),
        in_specs=[
            pl.BlockSpec((kv_rows, MOBA_BLOCK), lambda h, i: (h, i)),
            pl.BlockSpec((None, n_blk, MOBA_BLOCK, 2 * HEAD_DIM), lambda h, i: (h, 0, 0, 0)),
            pl.BlockSpec((n_blk, VT_ROWS, MOBA_BLOCK), lambda h, i: (0, h, 0)),
            pl.BlockSpec((n_blk, HEAD_DIM), lambda h, i: (0, h)),
            pl.BlockSpec((MOBA_BLOCK, kv_rows), lambda h, i: (i, h)),
        ],
        out_specs=pl.BlockSpec((MOBA_BLOCK, kv_rows), lambda h, i: (i, h)),
        scratch_shapes=[pltpu.VMEM((2 * HEAD_DIM, Q_PER_KV * MOBA_BLOCK), BF16),
                        stat_buf,
                        pltpu.VMEM((Q_PER_KV, VT_ROWS, MOBA_BLOCK), F32),
                        logit_buf, stat_buf, logit_buf, stat_buf],
        compiler_params=_cparams("parallel", "parallel"), name="moba_prompt",
    )(qT, ka, vT, ksum, gz)


def _ssd_prompt_kernel(xbc_ref, z_ref, dt_ref, convw_ref, convb_ref, dtb_ref, acoef_ref, dskipe_ref,
                       ng_ref, o_ref, state_ref, xpad_ref, st_ref, y_ref):
    c = pl.program_id(0)
    t = SSD_CHUNK

    @pl.when(c == 0)
    def _():
        xpad_ref[:SUBLANES, :] = jnp.zeros((SUBLANES, CONV_DIM), F32)
        st_ref[...] = jnp.zeros_like(st_ref)

    xpad_ref[SUBLANES:, :] = xbc_ref[...]
    conv = convb_ref[...] + convw_ref[CONV_WIDTH - 1:CONV_WIDTH, :] * xpad_ref[SUBLANES:, :]
    for w in range(CONV_WIDTH - 1):
        back = CONV_WIDTH - 1 - w
        conv = conv + convw_ref[w:w + 1, :] * xpad_ref[SUBLANES - back:SUBLANES - back + t, :]
    xpad_ref[:SUBLANES, :] = xpad_ref[t:t + SUBLANES, :]
    xc = _silu(conv)
    xs = xc[:, :SSD_WIDTH]

    dt = jax.nn.softplus(dt_ref[...] + dtb_ref[...])
    head_of_channel = lax.broadcasted_iota(jnp.int32, (LANES, SSD_WIDTH), 1) // SSD_HEAD_DIM
    expand = (head_of_channel == lax.broadcasted_iota(jnp.int32, (LANES, SSD_WIDTH), 0)).astype(BF16)
    dt_e = _dot3(_split3(dt), expand)
    lower = lax.broadcasted_iota(jnp.int32, (t, t), 0) >= lax.broadcasted_iota(jnp.int32, (t, t), 1)
    tril = lower.astype(BF16)
    cumsum = lambda v: _dot3(_split3(v), tril, left=True)
    a_cum = cumsum(dt * acoef_ref[...])
    a_cum_t = a_cum.T
    a_cum_e = _dot3(_split3(a_cum), expand)
    a_end_e = a_cum_e[t - 1:t, :]

    xdt = xs * dt_e
    xdt_end = (xdt * jnp.exp(a_end_e - a_cum_e)).astype(BF16)
    from_start = jnp.exp(a_cum_e)
    state_decay = jnp.exp(a_end_e)
    lane = lax.broadcasted_iota(jnp.int32, (t, LANES), 1)

    for g in range(SSD_GROUPS):
        gc = slice(g * GROUP_WIDTH, (g + 1) * GROUP_WIDTH)
        b_bf = xc[:, SSD_WIDTH + g * D_STATE:SSD_WIDTH + (g + 1) * D_STATE].astype(BF16)
        c_bf = xc[:, SSD_WIDTH + (SSD_GROUPS + g) * D_STATE:
                  SSD_WIDTH + (SSD_GROUPS + g + 1) * D_STATE].astype(BF16)
        cb = lax.dot_general(c_bf, b_bf, NT_DIMS, preferred_element_type=F32)
        st = st_ref[g]
        y_ref[:, gc] = jnp.dot(c_bf, st.astype(BF16), preferred_element_type=F32) * from_start[:, gc]
        st_ref[g] = st * state_decay[:, gc] + lax.dot_general(b_bf, xdt_end[:, gc], TN_DIMS,
                                                               preferred_element_type=F32)
        for pair in range(GROUP_WIDTH // LANES):
            pc = slice(g * GROUP_WIDTH + pair * LANES, g * GROUP_WIDTH + (pair + 1) * LANES)
            xdt_pair = xdt[:, pc]
            y_pair = y_ref[:, pc] + dskipe_ref[:, pc] * xs[:, pc]
            for hh in range(LANES // SSD_HEAD_DIM):
                h = (g * GROUP_WIDTH + pair * LANES) // SSD_HEAD_DIM + hh
                decay = jnp.exp(jnp.where(lower, a_cum[:, h:h + 1] - a_cum_t[h:h + 1, :], -jnp.inf))
                mine = (lane // SSD_HEAD_DIM) == hh
                rhs = jnp.where(mine, xdt_pair, 0.0).astype(BF16)
                y_pair = y_pair + jnp.dot((cb * decay).astype(BF16), rhs, preferred_element_type=F32)
            y_ref[:, pc] = y_pair

    y = y_ref[...] * _silu(z_ref[...])
    o_ref[...] = _rmsnorm(y, ng_ref[...]).astype(o_ref.dtype)

    @pl.when(c == pl.num_programs(0) - 1)
    def _():
        for g in range(SSD_GROUPS):
            for pair in range(GROUP_WIDTH // LANES):
                both = st_ref[g][:, pair * LANES:(pair + 1) * LANES].T
                for hh in range(LANES // SSD_HEAD_DIM):
                    h = (g * GROUP_WIDTH + pair * LANES) // SSD_HEAD_DIM + hh
                    state_ref[h] = both[hh * SSD_HEAD_DIM:(hh + 1) * SSD_HEAD_DIM, :]


def _ssd_prompt(xbc, gz, dt_raw, conv_w, conv_b, dt_bias, a_log, d_skip, ssd_norm_g):
    n = xbc.shape[0]
    t = SSD_CHUNK
    assert n % t == 0
    pad = lambda v: jnp.pad(v.reshape(1, N_SSD_HEADS), ((0, 0), (0, LANES - N_SSD_HEADS)))
    per_channel = lambda v: jnp.repeat(v.astype(F32), SSD_HEAD_DIM).reshape(1, SSD_WIDTH)
    a_coef = -jnp.exp(a_log.astype(F32))
    row = lambda c: (c, 0)
    return pl.pallas_call(
        _ssd_prompt_kernel,
        out_shape=(jax.ShapeDtypeStruct((n, SSD_WIDTH), BF16),
                   jax.ShapeDtypeStruct((N_SSD_HEADS, SSD_HEAD_DIM, D_STATE), F32)),
        grid=(n // t,),
        in_specs=[pl.BlockSpec((t, CONV_DIM), row),
                  pl.BlockSpec((t, SSD_WIDTH), lambda c: (c, 1)),
                  pl.BlockSpec((t, LANES), row),
                  _resident((CONV_WIDTH, CONV_DIM)), _resident((1, CONV_DIM)), _resident((1, LANES)),
                  _resident((1, LANES)), _resident((1, SSD_WIDTH)), _resident((1, SSD_WIDTH))],
        out_specs=(pl.BlockSpec((t, SSD_WIDTH), row),
                   pl.BlockSpec((N_SSD_HEADS, SSD_HEAD_DIM, D_STATE), lambda c: (0, 0, 0))),
        scratch_shapes=[pltpu.VMEM((t + SUBLANES, CONV_DIM), F32),
                        pltpu.VMEM((SSD_GROUPS, D_STATE, GROUP_WIDTH), F32),
                        pltpu.VMEM((t, SSD_WIDTH), F32)],
        compiler_params=_cparams("arbitrary"), name="ssd_prompt",
    )(xbc, gz, dt_raw, conv_w, conv_b.reshape(1, CONV_DIM), pad(dt_bias), pad(a_coef), per_channel(d_skip),
      ssd_norm_g.reshape(1, SSD_WIDTH))


def _out_proj_kernel(x_ref, oa_ref, os_ref, w_ref, fg_ref, h_ref, *, final_norm):
    out = jnp.dot(oa_ref[...], w_ref[:ATT_WIDTH, :], preferred_element_type=F32)
    out = out + jnp.dot(os_ref[...], w_ref[ATT_WIDTH:, :], preferred_element_type=F32)
    h = x_ref[...] + out
    h_ref[...] = _rmsnorm(h, fg_ref[...]) if final_norm else h


def _out_proj(x, o_att, o_ssd, w_out_bf, final_g, final_norm, tm):
    n = x.shape[0]
    assert n % tm == 0
    row = lambda i: (i, 0)
    return pl.pallas_call(
        functools.partial(_out_proj_kernel, final_norm=final_norm),
        out_shape=jax.ShapeDtypeStruct((n, D_MODEL), F32),
        grid=(n // tm,),
        in_specs=[pl.BlockSpec((tm, D_MODEL), row), pl.BlockSpec((tm, ATT_WIDTH), row),
                  pl.BlockSpec((tm, SSD_WIDTH), row), _resident((MIX_WIDTH, D_MODEL)), _resident((1, D_MODEL))],
        out_specs=pl.BlockSpec((tm, D_MODEL), row),
        compiler_params=_cparams("parallel"), name="out_proj",
    )(x, o_att, o_ssd, w_out_bf, final_g.reshape(1, D_MODEL))


def _prompt_layer(x, norm_g, w_in, conv_w, conv_b, dt_bias, a_log, d_skip, ssd_norm_g, w_out, final_g,
                  final_norm, tm):
    n = x.shape[0]
    qT, k, v, ka, vT, gz, xbc, dt_raw, ksum = _in_proj(x, norm_g, w_in, tm)
    o_att = _moba_prompt(qT, ka, vT, ksum.reshape(n // MOBA_BLOCK, KV_WIDTH), gz)
    o_ssd, state = _ssd_prompt(xbc, gz, dt_raw, conv_w, conv_b, dt_bias, a_log, d_skip, ssd_norm_g)
    h = _out_proj(x, o_att, o_ssd, w_out.astype(BF16), final_g, final_norm, tm)
    return h, k, v, xbc[n - (CONV_WIDTH - 1):], state


PAGES_PER_BLOCK = MOBA_BLOCK // PAGE_SIZE
PAGE_ROWS = PAGE_SIZE * N_KV_HEADS
SAMPLE_U_WIDTH = XBC_END + SSD_WIDTH


def _in_proj_sample_kernel(x_ref, g_ref, wqT_ref, wkv_ref, wrest_ref, wdte_ref, u_ref):
    xn = _rmsnorm(x_ref[...], g_ref[...]).astype(BF16)
    u_ref[:, :Q_END] = lax.dot_general(xn, wqT_ref[...], NT_DIMS, preferred_element_type=F32)
    u_ref[:, Q_END:V_END] = jnp.dot(xn, wkv_ref[...], preferred_element_type=F32)
    u_ref[:, V_END:XBC_END] = jnp.dot(xn, wrest_ref[...], preferred_element_type=F32)
    u_ref[:, XBC_END:] = jnp.dot(xn, wdte_ref[...], preferred_element_type=F32)


def _in_proj_sample(x, norm_g, weights):
    n = x.shape[0]
    wqT, wkv, _, wrest, wdt = weights
    wdte = jnp.repeat(wdt[:, :N_SSD_HEADS], SSD_HEAD_DIM, axis=1)
    return pl.pallas_call(
        _in_proj_sample_kernel, out_shape=jax.ShapeDtypeStruct((n, SAMPLE_U_WIDTH), F32), grid=(1,),
        in_specs=[_resident((n, D_MODEL)), _resident((1, D_MODEL)), _resident(wqT.shape), _resident(wkv.shape),
                  _resident(wrest.shape), _resident(wdte.shape)],
        out_specs=pl.BlockSpec((n, SAMPLE_U_WIDTH), lambda i: (0, 0)),
        compiler_params=_cparams("arbitrary"), name="in_proj_sample",
    )(x, norm_g.reshape(1, D_MODEL), wqT, wkv, wrest, wdte)


def _kbar_sample_kernel(pt_ref, cache_ref, out_ref, buf_ref, sem_ref, *, layer):
    b = pl.program_id(0)
    n_pages = pt_ref.shape[1]
    slot = b % 2

    def copies(seq, slot):
        return [pltpu.make_async_copy(cache_ref.at[layer, pt_ref[seq, p]], buf_ref.at[slot, p], sem_ref.at[slot])
                for p in range(n_pages)]

    @pl.when(b == 0)
    def _():
        for cp in copies(0, 0):
            cp.start()

    @pl.when(b + 1 < pl.num_programs(0))
    def _():
        for cp in copies(b + 1, 1 - slot):
            cp.start()

    for cp in copies(b, slot):
        cp.wait()
    for blk in range(n_pages // PAGES_PER_BLOCK):
        acc = jnp.zeros((SUBLANES, HEAD_DIM), F32)
        for p in range(PAGES_PER_BLOCK):
            page = buf_ref[slot, blk * PAGES_PER_BLOCK + p]
            acc = acc + jnp.sum(page.reshape(PAGE_ROWS // SUBLANES, SUBLANES, HEAD_DIM), axis=0)
        acc = acc + pltpu.roll(acc, 2, 0)
        acc = acc + pltpu.roll(acc, 4, 0)
        for h in range(N_KV_HEADS):
            out_ref[0, h, blk:blk + 1, :] = acc[h:h + 1, :] * (1.0 / MOBA_BLOCK)


def _kbar_sample(cache_pages, page_table, layer):
    n_seq, n_pages = page_table.shape
    assert n_pages % PAGES_PER_BLOCK == 0
    n_blk = n_pages // PAGES_PER_BLOCK
    return pl.pallas_call(
        functools.partial(_kbar_sample_kernel, layer=layer),
        out_shape=jax.ShapeDtypeStruct((n_seq, N_KV_HEADS, n_blk, HEAD_DIM), F32),
        grid_spec=pltpu.PrefetchScalarGridSpec(
            num_scalar_prefetch=1, grid=(n_seq,),
            in_specs=[pl.BlockSpec(memory_space=pl.ANY)],
            out_specs=pl.BlockSpec((1, N_KV_HEADS, n_blk, HEAD_DIM), lambda b, pt: (b, 0, 0, 0)),
            scratch_shapes=[pltpu.VMEM((2, n_pages, PAGE_ROWS, HEAD_DIM), F32),
                            pltpu.SemaphoreType.DMA((2,))]),
        compiler_params=_cparams("arbitrary"), name="kbar_sample",
    )(page_table, cache_pages)


def _gate_sample_kernel(q_ref, kbar_ref, idx_ref, gate_ref):
    n_seq, _, n_blk, _ = kbar_ref.shape
    for b in range(n_seq):
        for kh in range(N_KV_HEADS):
            r0 = b * N_ATT_HEADS + kh * Q_PER_KV
            q4 = q_ref[r0:r0 + Q_PER_KV, :].astype(BF16)
            gate_ref[r0:r0 + Q_PER_KV, :] = lax.dot_general(
                q4, kbar_ref[b, kh].astype(BF16), NT_DIMS, preferred_element_type=F32)
    gate = gate_ref[...]
    lane = lax.broadcasted_iota(jnp.int32, gate.shape, 1)
    idx_ref[...] = jnp.zeros_like(idx_ref)
    for r in range(MOBA_TOPK):
        best = jnp.max(gate, axis=1, keepdims=True)
        first = jnp.min(jnp.where(gate == best, lane, n_blk), axis=1, keepdims=True)
        idx_ref[:, r:r + 1] = first
        gate = jnp.where(lane == first, -jnp.inf, gate)


def _gate_sample(q_rows, kbar):
    n_seq, _, n_blk, _ = kbar.shape
    assert n_blk >= MOBA_TOPK
    rows = n_seq * N_ATT_HEADS
    return pl.pallas_call(
        _gate_sample_kernel, out_shape=jax.ShapeDtypeStruct((rows, LANES), jnp.int32), grid=(1,),
        in_specs=[_resident((rows, HEAD_DIM)), _resident(kbar.shape)],
        out_specs=pl.BlockSpec((rows, LANES), lambda i: (0, 0)),
        scratch_shapes=[pltpu.VMEM((rows, n_blk), F32)],
        compiler_params=_cparams("arbitrary"), name="gate_sample",
    )(q_rows, kbar)


def _attn_sample_kernel(pt_ref, idx_ref, q_ref, knew_ref, vnew_ref, g_ref, ck_ref, cv_ref, o_ref,
                        kbuf_ref, vbuf_ref, sem_ref, *, layer):
    b = pl.program_id(0)
    n_sel = N_ATT_HEADS * MOBA_TOPK
    cur = b % 2

    def copies(seq, buf):
        out = []
        for s in range(n_sel):
            blk = idx_ref[seq * n_sel + s]
            for p in range(PAGES_PER_BLOCK):
                page = pt_ref[seq, blk * PAGES_PER_BLOCK + p]
                slot = s * PAGES_PER_BLOCK + p
                kh = (s // MOBA_TOPK) // Q_PER_KV
                out.append(pltpu.make_async_copy(ck_ref.at[layer, page, :, kh, :], kbuf_ref.at[buf, slot],
                                                 sem_ref.at[buf, 0]))
                out.append(pltpu.make_async_copy(cv_ref.at[layer, page, :, kh, :], vbuf_ref.at[buf, slot],
                                                 sem_ref.at[buf, 1]))
        return out

    @pl.when(b == 0)
    def _():
        for cp in copies(0, 0):
            cp.start()

    @pl.when(b + 1 < pl.num_programs(0))
    def _():
        for cp in copies(b + 1, 1 - cur):
            cp.start()

    for cp in copies(b, cur):
        cp.wait()

    for h in range(N_ATT_HEADS):
        kh = h // Q_PER_KV
        qh = q_ref[0, h:h + 1, :].astype(BF16)
        q8 = jnp.broadcast_to(qh, (SUBLANES, HEAD_DIM))
        scores = []
        for s in range(MOBA_TOPK * PAGES_PER_BLOCK):
            page = kbuf_ref[cur, h * MOBA_TOPK * PAGES_PER_BLOCK + s].astype(BF16)
            sc = lax.dot_general(q8, page, NT_DIMS, preferred_element_type=F32)[0:1, :]
            scores.append(sc)
        k_self = knew_ref[0, kh:kh + 1, :].astype(BF16).astype(F32)
        s_self = jnp.sum(qh.astype(F32) * k_self, axis=1, keepdims=True)
        m = s_self
        for sc in scores:
            m = jnp.maximum(m, jnp.max(sc, axis=1, keepdims=True))
        p_self = jnp.exp((s_self - m) * ATT_SCALE)
        l = p_self
        acc = p_self.astype(BF16).astype(F32) * vnew_ref[0, kh:kh + 1, :].astype(BF16).astype(F32)
        for s, sc in enumerate(scores):
            p = jnp.exp((sc - m) * ATT_SCALE)
            l = l + jnp.sum(p, axis=1, keepdims=True)
            p8 = jnp.broadcast_to(p.astype(BF16), (SUBLANES, PAGE_SIZE))
            page = vbuf_ref[cur, h * MOBA_TOPK * PAGES_PER_BLOCK + s].astype(BF16)
            acc = acc + jnp.dot(p8, page, preferred_element_type=F32)[0:1, :]
        o_ref[0, h:h + 1, :] = ((acc / l) * _silu(g_ref[0, h:h + 1, :])).astype(o_ref.dtype)


def _attn_sample(q3, knew3, vnew3, g3, cache_k_pages, cache_v_pages, page_table, idx_flat, layer):
    n_seq = q3.shape[0]
    n_slots = N_ATT_HEADS * MOBA_TOPK * PAGES_PER_BLOCK
    per_seq = lambda r: pl.BlockSpec((1, r, HEAD_DIM), lambda b, pt, ix: (b, 0, 0))
    return pl.pallas_call(
        functools.partial(_attn_sample_kernel, layer=layer),
        out_shape=jax.ShapeDtypeStruct((n_seq, N_ATT_HEADS, HEAD_DIM), BF16),
        grid_spec=pltpu.PrefetchScalarGridSpec(
            num_scalar_prefetch=2, grid=(n_seq,),
            in_specs=[per_seq(N_ATT_HEADS), per_seq(N_KV_HEADS), per_seq(N_KV_HEADS), per_seq(N_ATT_HEADS),
                      pl.BlockSpec(memory_space=pl.ANY), pl.BlockSpec(memory_space=pl.ANY)],
            out_specs=per_seq(N_ATT_HEADS),
            scratch_shapes=[pltpu.VMEM((2, n_slots, PAGE_SIZE, HEAD_DIM), F32),
                            pltpu.VMEM((2, n_slots, PAGE_SIZE, HEAD_DIM), F32),
                            pltpu.SemaphoreType.DMA((2, 2))]),
        compiler_params=_cparams("arbitrary"), name="attn_sample",
    )(page_table, idx_flat, q3, knew3, vnew3, g3, cache_k_pages, cache_v_pages)


def _column(row):
    return jnp.broadcast_to(row, (LANES, LANES)).T


def _ssd_sample_kernel(xbc_ref, z_ref, dt_ref, sc_ref, ss_ref, convw_ref, convb_ref, dtb_ref, acoef_ref,
                       dskip_ref, ng_ref, o_ref, nc_ref, ns_ref, y_ref):
    hist = sc_ref[0]
    new = xbc_ref[0]
    conv = convb_ref[...] + convw_ref[CONV_WIDTH - 1:CONV_WIDTH, :] * new
    for w in range(CONV_WIDTH - 1):
        conv = conv + convw_ref[w:w + 1, :] * hist[w:w + 1, :]
    nc_ref[0, :CONV_WIDTH - 2, :] = hist[1:, :]
    nc_ref[0, CONV_WIDTH - 2:, :] = new
    xc = _silu(conv)
    xs = xc[:, :SSD_WIDTH]
    dt = jax.nn.softplus(dt_ref[0] + dtb_ref[...])
    d_a = jnp.exp(dt * acoef_ref[...])
    xdt = xs * dt
    heads_per_chunk = LANES // SSD_HEAD_DIM
    for c in range(SSD_WIDTH // LANES):
        lanes = slice(c * LANES, (c + 1) * LANES)
        x_col = _column(xdt[:, lanes])
        da_col = _column(d_a[:, lanes])
        y_cols = []
        for hh in range(heads_per_chunk):
            h = c * heads_per_chunk + hh
            g = h // HEADS_PER_GROUP
            rows = slice(hh * SSD_HEAD_DIM, (hh + 1) * SSD_HEAD_DIM)
            b_g = xc[:, SSD_WIDTH + g * D_STATE:SSD_WIDTH + (g + 1) * D_STATE]
            c_g = xc[:, SSD_WIDTH + (SSD_GROUPS + g) * D_STATE:SSD_WIDTH + (SSD_GROUPS + g + 1) * D_STATE]
            st = ss_ref[0, h] * da_col[rows, :] + x_col[rows, :] * b_g
            ns_ref[0, h] = st
            y_cols.append(jnp.sum(st * c_g, axis=1, keepdims=True))
        y_col = jnp.concatenate(y_cols, axis=0)
        y_ref[:, lanes] = jnp.broadcast_to(y_col, (LANES, LANES)).T[0:1, :]
    y = (y_ref[...] + dskip_ref[...] * xs) * _silu(z_ref[0])
    o_ref[0] = _rmsnorm(y, ng_ref[...]).astype(o_ref.dtype)


def _ssd_sample(xbc3, z3, dte3, state_conv, state_ssm, layer, conv_w, conv_b, dt_bias, a_log, d_skip, ssd_norm_g):
    n_seq = xbc3.shape[0]
    per_channel = lambda v: jnp.repeat(v.astype(F32), SSD_HEAD_DIM).reshape(1, SSD_WIDTH)
    seq3 = lambda w: pl.BlockSpec((1, 1, w), lambda b: (b, 0, 0))
    state_spec = pl.BlockSpec((None, 1, N_SSD_HEADS, SSD_HEAD_DIM, D_STATE), lambda b: (layer, b, 0, 0, 0))
    in_specs = [seq3(CONV_DIM), seq3(SSD_WIDTH), seq3(SSD_WIDTH),
                pl.BlockSpec((None, 1, CONV_WIDTH - 1, CONV_DIM), lambda b: (layer, b, 0, 0)), state_spec,
                _resident((CONV_WIDTH, CONV_DIM)), _resident((1, CONV_DIM)), _resident((1, SSD_WIDTH)),
                _resident((1, SSD_WIDTH)), _resident((1, SSD_WIDTH)), _resident((1, SSD_WIDTH))]
    operands = [xbc3, z3, dte3, state_conv, state_ssm, conv_w, conv_b.reshape(1, CONV_DIM), per_channel(dt_bias),
                per_channel(-jnp.exp(a_log.astype(F32))), per_channel(d_skip), ssd_norm_g.reshape(1, SSD_WIDTH)]
    return pl.pallas_call(
        _ssd_sample_kernel,
        out_shape=(jax.ShapeDtypeStruct((n_seq, 1, SSD_WIDTH), BF16),
                   jax.ShapeDtypeStruct((n_seq, CONV_WIDTH - 1, CONV_DIM), F32),
                   jax.ShapeDtypeStruct((n_seq, N_SSD_HEADS, SSD_HEAD_DIM, D_STATE), F32)),
        grid=(n_seq,), in_specs=in_specs,
        out_specs=(seq3(SSD_WIDTH), pl.BlockSpec((1, CONV_WIDTH - 1, CONV_DIM), lambda b: (b, 0, 0)),
                   pl.BlockSpec((1, N_SSD_HEADS, SSD_HEAD_DIM, D_STATE), lambda b: (b, 0, 0, 0))),
        scratch_shapes=[pltpu.VMEM((1, SSD_WIDTH), F32)],
        compiler_params=_cparams("parallel"), name="ssd_sample",
    )(*operands)


def _sample_layer(x, layer, cache_k, cache_v, state_conv, state_ssm, page_table,
                  norm_g, w_in, conv_w, conv_b, dt_bias, a_log, d_skip, ssd_norm_g, w_out, final_g, final_norm):
    n_seq = x.shape[0]
    u = _in_proj_sample(x, norm_g, w_in)
    q, k, v, g_att = u[:, :Q_END], u[:, Q_END:K_END], u[:, K_END:V_END], u[:, V_END:G_END]
    z, xbc, dte = u[:, G_END:Z_END], u[:, Z_END:XBC_END], u[:, XBC_END:]
    cache_k_pages = cache_k.reshape(cache_k.shape[0], cache_k.shape[1], PAGE_ROWS, HEAD_DIM)
    kbar = _kbar_sample(cache_k_pages, page_table, layer)
    idx = _gate_sample(q.reshape(n_seq * N_ATT_HEADS, HEAD_DIM), kbar)
    heads = lambda a, r: a.reshape(n_seq, r, HEAD_DIM)
    o_att = _attn_sample(heads(q, N_ATT_HEADS), heads(k, N_KV_HEADS), heads(v, N_KV_HEADS),
                         heads(g_att, N_ATT_HEADS), cache_k, cache_v, page_table,
                         idx[:, :MOBA_TOPK].reshape(-1), layer)
    rows3 = lambda a: a.reshape(n_seq, 1, a.shape[1])
    o_ssd, new_conv, new_ssm = _ssd_sample(rows3(xbc), rows3(z), rows3(dte), state_conv, state_ssm, layer,
                                           conv_w, conv_b, dt_bias, a_log, d_skip, ssd_norm_g)
    h = _out_proj(x, o_att.reshape(n_seq, ATT_WIDTH), o_ssd.reshape(n_seq, SSD_WIDTH), w_out.astype(BF16),
                  final_g, final_norm, n_seq)
    return h, k, v, new_conv, new_ssm


PROMPT_ROW_TILE = 512


def kernel(x_prompt, x_sample, cache_k, cache_v, state_conv, state_ssm, page_table, norm_g, w_in, conv_w,
           conv_b, dt_bias, a_log, d_skip, ssd_norm_g, w_out, final_norm_g):
    depth = w_in.shape[0]
    batch, seq, _ = x_prompt.shape
    n_seq, dec_seq, _ = x_sample.shape
    assert batch == 1 and dec_seq == 1
    hp, hs = x_prompt[0], x_sample[:, 0]
    outs = [[] for _ in range(8)]
    for l in range(depth):
        last = l == depth - 1
        params = (norm_g[l], _in_proj_weights(w_in[l]), conv_w[l], conv_b[l], dt_bias[l], a_log[l], d_skip[l], ssd_norm_g[l],
                  w_out[l], final_norm_g, last)
        hp, kp, vp, cp, sp = _prompt_layer(hp, *params, PROMPT_ROW_TILE)
        hs, ks, vs, cs, ss = _sample_layer(hs, l, cache_k, cache_v, state_conv, state_ssm, page_table, *params)
        kv4 = lambda a: a.reshape(n_seq, dec_seq, N_KV_HEADS, HEAD_DIM)
        for lst, val in zip(outs, (kp[None], vp[None], cp[None], sp[None], kv4(ks), kv4(vs), cs, ss)):
            lst.append(val)
    return (hp[None], hs[:, None]) + tuple(jnp.stack(o) for o in outs)
```

```python
import functools

import jax
import jax.numpy as jnp
from jax import lax
from jax.experimental import pallas as pl
from jax.experimental.pallas import tpu as pltpu

F32 = jnp.float32
BF16 = jnp.bfloat16

D_MODEL = 1024
N_ATT_HEADS = 8
N_KV_HEADS = 2
HEAD_DIM = 128
Q_PER_KV = N_ATT_HEADS // N_KV_HEADS
ATT_WIDTH = N_ATT_HEADS * HEAD_DIM
KV_WIDTH = N_KV_HEADS * HEAD_DIM
MOBA_BLOCK = 256
MOBA_TOPK = 3
PAGE_SIZE = 128
SSD_WIDTH = 1024
SSD_HEAD_DIM = 64
N_SSD_HEADS = SSD_WIDTH // SSD_HEAD_DIM
SSD_GROUPS = 2
HEADS_PER_GROUP = N_SSD_HEADS // SSD_GROUPS
GROUP_WIDTH = HEADS_PER_GROUP * SSD_HEAD_DIM
D_STATE = 128
CONV_WIDTH = 4
SSD_CHUNK = 256
CONV_DIM = SSD_WIDTH + 2 * SSD_GROUPS * D_STATE
MIX_WIDTH = ATT_WIDTH + SSD_WIDTH
Q_END = ATT_WIDTH
K_END = Q_END + KV_WIDTH
V_END = K_END + KV_WIDTH
G_END = V_END + ATT_WIDTH
Z_END = G_END + SSD_WIDTH
XBC_END = Z_END + CONV_DIM
IN_WIDTH = XBC_END + N_SSD_HEADS
NORM_EPS = 1e-5
ATT_SCALE = HEAD_DIM ** -0.5
EXP2_SCALE = ATT_SCALE * 1.4426950408889634

LANES = 128
SUBLANES = 8
VMEM_LIMIT_BYTES = 56 * 1024 * 1024

MASK_BIAS = -1e30
VT_ONES = 16
VT_ROWS = HEAD_DIM + VT_ONES
KEY_BLOCKS_PER_STEP = 2

NT_DIMS = (((1,), (1,)), ((), ()))
TN_DIMS = (((0,), (0,)), ((), ()))


def _cparams(*semantics):
    return pltpu.CompilerParams(dimension_semantics=semantics, vmem_limit_bytes=VMEM_LIMIT_BYTES)


def _resident(shape):
    nd = len(shape)
    return pl.BlockSpec(shape, lambda *_: (0,) * nd, pipeline_mode=pl.Buffered(1))


def _rmsnorm(x, g):
    return x * lax.rsqrt(jnp.mean(x * x, axis=-1, keepdims=True) + NORM_EPS) * g


def _silu(x):
    return x * jax.nn.sigmoid(x)


def _reduce_rows(x, op, final):
    slabs = [x[r:r + SUBLANES] for r in range(0, x.shape[0], SUBLANES)]
    while len(slabs) > 1:
        nxt = [op(slabs[a], slabs[a + 1]) for a in range(0, len(slabs) - 1, 2)]
        if len(slabs) % 2:
            nxt.append(slabs[-1])
        slabs = nxt
    return final(slabs[0], axis=0, keepdims=True)


def _split3(x):
    hi = x.astype(BF16)
    r = x - hi.astype(F32)
    mid = r.astype(BF16)
    lo = (r - mid.astype(F32)).astype(BF16)
    return hi, mid, lo


def _dot3(parts, m, left=False):
    if left:
        f = lambda p: jnp.dot(m, p, preferred_element_type=F32)
    else:
        f = lambda p: jnp.dot(p, m, preferred_element_type=F32)
    return (f(parts[2]) + f(parts[1])) + f(parts[0])


def _in_proj_kernel(x_ref, g_ref, wqT_ref, wkv_ref, wvT_ref, wrest_ref, wdt_ref,
                    qT_ref, k_ref, v_ref, ka_ref, vT_ref, gz_ref, xbc_ref, dt_ref, ksum_ref):
    tm = x_ref.shape[0]
    xn = _rmsnorm(x_ref[...], g_ref[...]).astype(BF16)
    qT = lax.dot_general(wqT_ref[...], xn, NT_DIMS, preferred_element_type=F32)
    qT_ref[...] = (qT * EXP2_SCALE).astype(BF16)
    vT = lax.dot_general(wvT_ref[...], xn, NT_DIMS, preferred_element_type=F32).astype(BF16)
    kv = jnp.dot(xn, wkv_ref[...], preferred_element_type=F32)
    k = kv[:, :KV_WIDTH]
    for h in range(N_KV_HEADS):
        k_ref[:, h, :] = kv[:, h * HEAD_DIM:(h + 1) * HEAD_DIM]
        v_ref[:, h, :] = kv[:, KV_WIDTH + h * HEAD_DIM:KV_WIDTH + (h + 1) * HEAD_DIM]
    blk0 = pl.program_id(0) * (tm // MOBA_BLOCK)
    lane = lax.broadcasted_iota(jnp.int32, (MOBA_BLOCK, LANES), 1)
    for b in range(tm // MOBA_BLOCK):
        rows = slice(b * MOBA_BLOCK, (b + 1) * MOBA_BLOCK)
        onehot = (lane == blk0 + b).astype(BF16)
        for h in range(N_KV_HEADS):
            ka_ref[h, b, :, :HEAD_DIM] = k[rows, h * HEAD_DIM:(h + 1) * HEAD_DIM].astype(BF16)
            ka_ref[h, b, :, HEAD_DIM:] = onehot
        for h in range(N_KV_HEADS):
            vT_ref[b, h * VT_ROWS:h * VT_ROWS + HEAD_DIM, :] = vT[h * HEAD_DIM:(h + 1) * HEAD_DIM, rows]
            vT_ref[b, h * VT_ROWS + HEAD_DIM:(h + 1) * VT_ROWS, :] = jnp.ones((VT_ONES, MOBA_BLOCK), BF16)
        ksum_ref[0, b:b + 1, :] = jnp.sum(k[rows], axis=0, keepdims=True)
    rest = jnp.dot(xn, wrest_ref[...], preferred_element_type=F32)
    gz_ref[...] = rest[:, :ATT_WIDTH + SSD_WIDTH]
    xbc_ref[...] = rest[:, ATT_WIDTH + SSD_WIDTH:]
    dt_ref[...] = jnp.dot(xn, wdt_ref[...], preferred_element_type=F32)


def _in_proj_weights(w_in):
    wb = w_in.astype(BF16)
    wqT = wb[:, :Q_END].T
    wkv = wb[:, Q_END:V_END]
    wvT = wb[:, K_END:V_END].T
    wrest = wb[:, V_END:XBC_END]
    wdt = jnp.pad(wb[:, XBC_END:], ((0, 0), (0, LANES - N_SSD_HEADS)))
    return wqT, wkv, wvT, wrest, wdt


def _in_proj(x, norm_g, weights, tm):
    n = x.shape[0]
    assert n % tm == 0 and tm % MOBA_BLOCK == 0
    wqT, wkv, wvT, wrest, wdt = weights
    nb = tm // MOBA_BLOCK
    row = lambda i: (i, 0)
    kv_spec = pl.BlockSpec((tm, N_KV_HEADS, HEAD_DIM), lambda i: (i, 0, 0))
    out_shape = (
        jax.ShapeDtypeStruct((ATT_WIDTH, n), BF16),
        jax.ShapeDtypeStruct((n, N_KV_HEADS, HEAD_DIM), F32),
        jax.ShapeDtypeStruct((n, N_KV_HEADS, HEAD_DIM), F32),
        jax.ShapeDtypeStruct((N_KV_HEADS, n // MOBA_BLOCK, MOBA_BLOCK, 2 * HEAD_DIM), BF16),
        jax.ShapeDtypeStruct((n // MOBA_BLOCK, N_KV_HEADS * VT_ROWS, MOBA_BLOCK), BF16),
        jax.ShapeDtypeStruct((n, ATT_WIDTH + SSD_WIDTH), F32),
        jax.ShapeDtypeStruct((n, CONV_DIM), F32),
        jax.ShapeDtypeStruct((n, LANES), F32),
        jax.ShapeDtypeStruct((n // tm, nb, KV_WIDTH), F32),
    )
    out_specs = (
        pl.BlockSpec((ATT_WIDTH, tm), lambda i: (0, i)),
        kv_spec,
        kv_spec,
        pl.BlockSpec((N_KV_HEADS, nb, MOBA_BLOCK, 2 * HEAD_DIM), lambda i: (0, i, 0, 0)),
        pl.BlockSpec((nb, N_KV_HEADS * VT_ROWS, MOBA_BLOCK), lambda i: (i, 0, 0)),
        pl.BlockSpec((tm, ATT_WIDTH + SSD_WIDTH), row),
        pl.BlockSpec((tm, CONV_DIM), row),
        pl.BlockSpec((tm, LANES), row),
        pl.BlockSpec((1, nb, KV_WIDTH), lambda i: (i, 0, 0)),
    )
    in_specs = [pl.BlockSpec((tm, D_MODEL), row), _resident((1, D_MODEL)), _resident(wqT.shape),
                _resident(wkv.shape), _resident(wvT.shape), _resident(wrest.shape), _resident(wdt.shape)]
    return pl.pallas_call(
        _in_proj_kernel, out_shape=out_shape, grid=(n // tm,), in_specs=in_specs, out_specs=out_specs,
        compiler_params=_cparams("parallel"), name="in_proj",
    )(x, norm_g.reshape(1, D_MODEL), wqT, wkv, wvT, wrest, wdt)


def _moba_prompt_kernel(qT_ref, ka_ref, vT_ref, ksum_ref, g_ref, o_ref, qa_ref, m_ref, acc_ref,
                        s0_ref, c0_ref, s1_ref, c1_ref):
    i = pl.program_id(1)
    n_blk = ksum_ref.shape[0]
    kbar = (ksum_ref[...] * (1.0 / MOBA_BLOCK)).astype(BF16)
    blk = lax.broadcasted_iota(jnp.int32, (n_blk, MOBA_BLOCK), 0)
    key_pos = lax.broadcasted_iota(jnp.int32, (MOBA_BLOCK, MOBA_BLOCK), 0)
    qry_pos = lax.broadcasted_iota(jnp.int32, (MOBA_BLOCK, MOBA_BLOCK), 1)

    for h in range(Q_PER_KV):
        cols = slice(h * MOBA_BLOCK, (h + 1) * MOBA_BLOCK)
        qT = qT_ref[h * HEAD_DIM:(h + 1) * HEAD_DIM, :]
        gate = jnp.dot(kbar, qT, preferred_element_type=F32)
        gate = jnp.where(blk < i, gate, -jnp.inf)
        bias = jnp.full((n_blk, MOBA_BLOCK), MASK_BIAS, F32)
        for r in range(MOBA_TOPK):
            best = jnp.max(gate, axis=0, keepdims=True)
            first = jnp.min(jnp.where(gate == best, blk, n_blk), axis=0, keepdims=True)
            first = jnp.where(r < i, first, -1)
            pick = blk == first
            bias = jnp.where(pick, 0.0, bias)
            gate = jnp.where(pick, -jnp.inf, gate)
        qa_ref[:HEAD_DIM, cols] = qT
        qa_ref[HEAD_DIM:HEAD_DIM + n_blk, cols] = bias.astype(BF16)
        if n_blk < LANES:
            qa_ref[HEAD_DIM + n_blk:, cols] = jnp.zeros((LANES - n_blk, MOBA_BLOCK), BF16)

    def own_block():
        k_own = ka_ref[i][:, :HEAD_DIM]
        logit = [jnp.dot(k_own, qT_ref[h * HEAD_DIM:(h + 1) * HEAD_DIM, :], preferred_element_type=F32)
                 for h in range(Q_PER_KV)]
        probs = []
        for h in range(Q_PER_KV):
            s = jnp.where(key_pos <= qry_pos, logit[h], MASK_BIAS)
            m_new = _reduce_rows(s, jnp.maximum, jnp.max)
            m_ref[h] = m_new
            probs.append(jnp.exp2(s - m_new).astype(BF16))
        for h in range(Q_PER_KV):
            acc_ref[h] = jnp.dot(vT_ref[i], probs[h], preferred_element_type=F32)

    kb = KEY_BLOCKS_PER_STEP
    n_groups = (i + kb - 1) // kb
    last_group = n_groups - 1

    def logits(group, dst_ref, dmax_ref):
        k_aug = ka_ref[pl.ds(kb * group, kb)].reshape(kb * MOBA_BLOCK, 2 * HEAD_DIM)
        for h in range(Q_PER_KV):
            s = jnp.dot(k_aug, qa_ref[:, h * MOBA_BLOCK:(h + 1) * MOBA_BLOCK],
                        preferred_element_type=F32)
            dst_ref[h] = s
            dmax_ref[h] = _reduce_rows(s, jnp.maximum, jnp.max)

    def softmax_step(group, cur_ref, cmax_ref, nxt_ref, nmax_ref):
        logits(jnp.minimum(group + 1, last_group), nxt_ref, nmax_ref)
        vT = jnp.concatenate([vT_ref[kb * group + b] for b in range(kb)], axis=1)
        for h in range(Q_PER_KV):
            m_old = m_ref[h]
            m_new = jnp.maximum(m_old, cmax_ref[h])
            p = jnp.exp2(cur_ref[h] - m_new).astype(BF16)
            acc_ref[h] = jnp.exp2(m_old - m_new) * acc_ref[h] + jnp.dot(vT, p, preferred_element_type=F32)
            m_ref[h] = m_new

    own_block()
    logits(0, s0_ref, c0_ref)

    def one_group(group, carry):
        @pl.when(group % 2 == 0)
        def _():
            softmax_step(group, s0_ref, c0_ref, s1_ref, c1_ref)

        @pl.when(group % 2 == 1)
        def _():
            softmax_step(group, s1_ref, c1_ref, s0_ref, c0_ref)

        return carry

    lax.fori_loop(0, n_groups, one_group, 0)

    for h in range(Q_PER_KV):
        acc = acc_ref[h]
        out = (acc[:HEAD_DIM] / acc[HEAD_DIM:HEAD_DIM + 1]).T
        dcols = slice(h * HEAD_DIM, (h + 1) * HEAD_DIM)
        o_ref[:, dcols] = (out * _silu(g_ref[:, dcols])).astype(BF16)


def _moba_prompt(qT, ka, vT, ksum, gz):
    n = qT.shape[1]
    n_blk = n // MOBA_BLOCK
    assert n_blk <= LANES and n_blk % KEY_BLOCKS_PER_STEP == 0
    kv_rows = Q_PER_KV * HEAD_DIM
    logit_buf = pltpu.VMEM((Q_PER_KV, KEY_BLOCKS_PER_STEP * MOBA_BLOCK, MOBA_BLOCK), F32)
    stat_buf = pltpu.VMEM((Q_PER_KV, 1, MOBA_BLOCK), F32)
    return pl.pallas_call(
        _moba_prompt_kernel,
        out_shape=jax.ShapeDtypeStruct((n, ATT_WIDTH), BF16),
        grid=(N_KV_HEADS, n_blk),
        in_specs=[
            pl.BlockSpec((kv_rows, MOBA_BLOCK), lambda h, i: (h, i)),
            pl.BlockSpec((None, n_blk, MOBA_BLOCK, 2 * HEAD_DIM), lambda h, i: (h, 0, 0, 0)),
            pl.BlockSpec((n_blk, VT_ROWS, MOBA_BLOCK), lambda h, i: (0, h, 0)),
            pl.BlockSpec((n_blk, HEAD_DIM), lambda h, i: (0, h)),
            pl.BlockSpec((MOBA_BLOCK, kv_rows), lambda h, i: (i, h)),
        ],
        out_specs=pl.BlockSpec((MOBA_BLOCK, kv_rows), lambda h, i: (i, h)),
        scratch_shapes=[pltpu.VMEM((2 * HEAD_DIM, Q_PER_KV * MOBA_BLOCK), BF16),
                        stat_buf,
                        pltpu.VMEM((Q_PER_KV, VT_ROWS, MOBA_BLOCK), F32),
                        logit_buf, stat_buf, logit_buf, stat_buf],
        compiler_params=_cparams("parallel", "parallel"), name="moba_prompt",
    )(qT, ka, vT, ksum, gz)


def _ssd_prompt_kernel(xbc_ref, z_ref, dt_ref, convw_ref, convb_ref, dtb_ref, acoef_ref, dskipe_ref,
                       ng_ref, o_ref, state_ref, xpad_ref, st_ref, y_ref):
    c = pl.program_id(0)
    t = SSD_CHUNK

    @pl.when(c == 0)
    def _():
        xpad_ref[:SUBLANES, :] = jnp.zeros((SUBLANES, CONV_DIM), F32)
        st_ref[...] = jnp.zeros_like(st_ref)

    xpad_ref[SUBLANES:, :] = xbc_ref[...]
    conv = convb_ref[...] + convw_ref[CONV_WIDTH - 1:CONV_WIDTH, :] * xpad_ref[SUBLANES:, :]
    for w in range(CONV_WIDTH - 1):
        back = CONV_WIDTH - 1 - w
        conv = conv + convw_ref[w:w + 1, :] * xpad_ref[SUBLANES - back:SUBLANES - back + t, :]
    xpad_ref[:SUBLANES, :] = xpad_ref[t:t + SUBLANES, :]
    xc = _silu(conv)
    xs = xc[:, :SSD_WIDTH]

    dt = jax.nn.softplus(dt_ref[...] + dtb_ref[...])
    head_of_channel = lax.broadcasted_iota(jnp.int32, (LANES, SSD_WIDTH), 1) // SSD_HEAD_DIM
    expand = (head_of_channel == lax.broadcasted_iota(jnp.int32, (LANES, SSD_WIDTH), 0)).astype(BF16)
    dt_e = _dot3(_split3(dt), expand)
    lower = lax.broadcasted_iota(jnp.int32, (t, t), 0) >= lax.broadcasted_iota(jnp.int32, (t, t), 1)
    tril = lower.astype(BF16)
    cumsum = lambda v: _dot3(_split3(v), tril, left=True)
    a_cum = cumsum(dt * acoef_ref[...])
    a_cum_t = a_cum.T
    a_cum_e = _dot3(_split3(a_cum), expand)
    a_end_e = a_cum_e[t - 1:t, :]

    xdt = xs * dt_e
    xdt_end = (xdt * jnp.exp(a_end_e - a_cum_e)).astype(BF16)
    from_start = jnp.exp(a_cum_e)
    state_decay = jnp.exp(a_end_e)
    lane = lax.broadcasted_iota(jnp.int32, (t, LANES), 1)

    for g in range(SSD_GROUPS):
        gc = slice(g * GROUP_WIDTH, (g + 1) * GROUP_WIDTH)
        b_bf = xc[:, SSD_WIDTH + g * D_STATE:SSD_WIDTH + (g + 1) * D_STATE].astype(BF16)
        c_bf = xc[:, SSD_WIDTH + (SSD_GROUPS + g) * D_STATE:
                  SSD_WIDTH + (SSD_GROUPS + g + 1) * D_STATE].astype(BF16)
        cb = lax.dot_general(c_bf, b_bf, NT_DIMS, preferred_element_type=F32)
        st = st_ref[g]
        y_ref[:, gc] = jnp.dot(c_bf, st.astype(BF16), preferred_element_type=F32) * from_start[:, gc]
        st_ref[g] = st * state_decay[:, gc] + lax.dot_general(b_bf, xdt_end[:, gc], TN_DIMS,
                                                               preferred_element_type=F32)
        for pair in range(GROUP_WIDTH // LANES):
            pc = slice(g * GROUP_WIDTH + pair * LANES, g * GROUP_WIDTH + (pair + 1) * LANES)
            xdt_pair = xdt[:, pc]
            y_pair = y_ref[:, pc] + dskipe_ref[:, pc] * xs[:, pc]
            for hh in range(LANES // SSD_HEAD_DIM):
                h = (g * GROUP_WIDTH + pair * LANES) // SSD_HEAD_DIM + hh
                decay = jnp.exp(jnp.where(lower, a_cum[:, h:h + 1] - a_cum_t[h:h + 1, :], -jnp.inf))
                mine = (lane // SSD_HEAD_DIM) == hh
                rhs = jnp.where(mine, xdt_pair, 0.0).astype(BF16)
                y_pair = y_pair + jnp.dot((cb * decay).astype(BF16), rhs, preferred_element_type=F32)
            y_ref[:, pc] = y_pair

    y = y_ref[...] * _silu(z_ref[...])
    o_ref[...] = _rmsnorm(y, ng_ref[...]).astype(o_ref.dtype)

    @pl.when(c == pl.num_programs(0) - 1)
    def _():
        for g in range(SSD_GROUPS):
            for pair in range(GROUP_WIDTH // LANES):
                both = st_ref[g][:, pair * LANES:(pair + 1) * LANES].T
                for hh in range(LANES // SSD_HEAD_DIM):
                    h = (g * GROUP_WIDTH + pair * LANES) // SSD_HEAD_DIM + hh
                    state_ref[h] = both[hh * SSD_HEAD_DIM:(hh + 1) * SSD_HEAD_DIM, :]


def _ssd_prompt(xbc, gz, dt_raw, conv_w, conv_b, dt_bias, a_log, d_skip, ssd_norm_g):
    n = xbc.shape[0]
    t = SSD_CHUNK
    assert n % t == 0
    pad = lambda v: jnp.pad(v.reshape(1, N_SSD_HEADS), ((0, 0), (0, LANES - N_SSD_HEADS)))
    per_channel = lambda v: jnp.repeat(v.astype(F32), SSD_HEAD_DIM).reshape(1, SSD_WIDTH)
    a_coef = -jnp.exp(a_log.astype(F32))
    row = lambda c: (c, 0)
    return pl.pallas_call(
        _ssd_prompt_kernel,
        out_shape=(jax.ShapeDtypeStruct((n, SSD_WIDTH), BF16),
                   jax.ShapeDtypeStruct((N_SSD_HEADS, SSD_HEAD_DIM, D_STATE), F32)),
        grid=(n // t,),
        in_specs=[pl.BlockSpec((t, CONV_DIM), row),
                  pl.BlockSpec((t, SSD_WIDTH), lambda c: (c, 1)),
                  pl.BlockSpec((t, LANES), row),
                  _resident((CONV_WIDTH, CONV_DIM)), _resident((1, CONV_DIM)), _resident((1, LANES)),
                  _resident((1, LANES)), _resident((1, SSD_WIDTH)), _resident((1, SSD_WIDTH))],
        out_specs=(pl.BlockSpec((t, SSD_WIDTH), row),
                   pl.BlockSpec((N_SSD_HEADS, SSD_HEAD_DIM, D_STATE), lambda c: (0, 0, 0))),
        scratch_shapes=[pltpu.VMEM((t + SUBLANES, CONV_DIM), F32),
                        pltpu.VMEM((SSD_GROUPS, D_STATE, GROUP_WIDTH), F32),
                        pltpu.VMEM((t, SSD_WIDTH), F32)],
        compiler_params=_cparams("arbitrary"), name="ssd_prompt",
    )(xbc, gz, dt_raw, conv_w, conv_b.reshape(1, CONV_DIM), pad(dt_bias), pad(a_coef), per_channel(d_skip),
      ssd_norm_g.reshape(1, SSD_WIDTH))


def _out_proj_kernel(x_ref, oa_ref, os_ref, w_ref, fg_ref, h_ref, *, final_norm):
    out = jnp.dot(oa_ref[...], w_ref[:ATT_WIDTH, :], preferred_element_type=F32)
    out = out + jnp.dot(os_ref[...], w_ref[ATT_WIDTH:, :], preferred_element_type=F32)
    h = x_ref[...] + out
    h_ref[...] = _rmsnorm(h, fg_ref[...]) if final_norm else h


def _out_proj(x, o_att, o_ssd, w_out_bf, final_g, final_norm, tm):
    n = x.shape[0]
    assert n % tm == 0
    row = lambda i: (i, 0)
    return pl.pallas_call(
        functools.partial(_out_proj_kernel, final_norm=final_norm),
        out_shape=jax.ShapeDtypeStruct((n, D_MODEL), F32),
        grid=(n // tm,),
        in_specs=[pl.BlockSpec((tm, D_MODEL), row), pl.BlockSpec((tm, ATT_WIDTH), row),
                  pl.BlockSpec((tm, SSD_WIDTH), row), _resident((MIX_WIDTH, D_MODEL)), _resident((1, D_MODEL))],
        out_specs=pl.BlockSpec((tm, D_MODEL), row),
        compiler_params=_cparams("parallel"), name="out_proj",
    )(x, o_att, o_ssd, w_out_bf, final_g.reshape(1, D_MODEL))


def _prompt_layer(x, norm_g, w_in, conv_w, conv_b, dt_bias, a_log, d_skip, ssd_norm_g, w_out, final_g,
                  final_norm, tm):
    n = x.shape[0]
    qT, k, v, ka, vT, gz, xbc, dt_raw, ksum = _in_proj(x, norm_g, w_in, tm)
    o_att = _moba_prompt(qT, ka, vT, ksum.reshape(n // MOBA_BLOCK, KV_WIDTH), gz)
    o_ssd, state = _ssd_prompt(xbc, gz, dt_raw, conv_w, conv_b, dt_bias, a_log, d_skip, ssd_norm_g)
    h = _out_proj(x, o_att, o_ssd, w_out.astype(BF16), final_g, final_norm, tm)
    return h, k, v, xbc[n - (CONV_WIDTH - 1):], state


PAGES_PER_BLOCK = MOBA_BLOCK // PAGE_SIZE
PAGE_ROWS = PAGE_SIZE * N_KV_HEADS
SAMPLE_U_WIDTH = XBC_END + SSD_WIDTH


def _in_proj_sample_kernel(x_ref, g_ref, wqT_ref, wkv_ref, wrest_ref, wdte_ref, u_ref):
    xn = _rmsnorm(x_ref[...], g_ref[...]).astype(BF16)
    u_ref[:, :Q_END] = lax.dot_general(xn, wqT_ref[...], NT_DIMS, preferred_element_type=F32)
    u_ref[:, Q_END:V_END] = jnp.dot(xn, wkv_ref[...], preferred_element_type=F32)
    u_ref[:, V_END:XBC_END] = jnp.dot(xn, wrest_ref[...], preferred_element_type=F32)
    u_ref[:, XBC_END:] = jnp.dot(xn, wdte_ref[...], preferred_element_type=F32)


def _in_proj_sample(x, norm_g, weights):
    n = x.shape[0]
    wqT, wkv, _, wrest, wdt = weights
    wdte = jnp.repeat(wdt[:, :N_SSD_HEADS], SSD_HEAD_DIM, axis=1)
    return pl.pallas_call(
        _in_proj_sample_kernel, out_shape=jax.ShapeDtypeStruct((n, SAMPLE_U_WIDTH), F32), grid=(1,),
        in_specs=[_resident((n, D_MODEL)), _resident((1, D_MODEL)), _resident(wqT.shape), _resident(wkv.shape),
                  _resident(wrest.shape), _resident(wdte.shape)],
        out_specs=pl.BlockSpec((n, SAMPLE_U_WIDTH), lambda i: (0, 0)),
        compiler_params=_cparams("arbitrary"), name="in_proj_sample",
    )(x, norm_g.reshape(1, D_MODEL), wqT, wkv, wrest, wdte)


def _kbar_sample_kernel(pt_ref, cache_ref, out_ref, buf_ref, sem_ref, *, layer):
    b = pl.program_id(0)
    n_pages = pt_ref.shape[1]
    slot = b % 2

    def copies(seq, slot):
        return [pltpu.make_async_copy(cache_ref.at[layer, pt_ref[seq, p]], buf_ref.at[slot, p], sem_ref.at[slot])
                for p in range(n_pages)]

    @pl.when(b == 0)
    def _():
        for cp in copies(0, 0):
            cp.start()

    @pl.when(b + 1 < pl.num_programs(0))
    def _():
        for cp in copies(b + 1, 1 - slot):
            cp.start()

    for cp in copies(b, slot):
        cp.wait()
    for blk in range(n_pages // PAGES_PER_BLOCK):
        acc = jnp.zeros((SUBLANES, HEAD_DIM), F32)
        for p in range(PAGES_PER_BLOCK):
            page = buf_ref[slot, blk * PAGES_PER_BLOCK + p]
            acc = acc + jnp.sum(page.reshape(PAGE_ROWS // SUBLANES, SUBLANES, HEAD_DIM), axis=0)
        acc = acc + pltpu.roll(acc, 2, 0)
        acc = acc + pltpu.roll(acc, 4, 0)
        for h in range(N_KV_HEADS):
            out_ref[0, h, blk:blk + 1, :] = acc[h:h + 1, :] * (1.0 / MOBA_BLOCK)


def _kbar_sample(cache_pages, page_table, layer):
    n_seq, n_pages = page_table.shape
    assert n_pages % PAGES_PER_BLOCK == 0
    n_blk = n_pages // PAGES_PER_BLOCK
    return pl.pallas_call(
        functools.partial(_kbar_sample_kernel, layer=layer),
        out_shape=jax.ShapeDtypeStruct((n_seq, N_KV_HEADS, n_blk, HEAD_DIM), F32),
        grid_spec=pltpu.PrefetchScalarGridSpec(
            num_scalar_prefetch=1, grid=(n_seq,),
            in_specs=[pl.BlockSpec(memory_space=pl.ANY)],
            out_specs=pl.BlockSpec((1, N_KV_HEADS, n_blk, HEAD_DIM), lambda b, pt: (b, 0, 0, 0)),
            scratch_shapes=[pltpu.VMEM((2, n_pages, PAGE_ROWS, HEAD_DIM), F32),
                            pltpu.SemaphoreType.DMA((2,))]),
        compiler_params=_cparams("arbitrary"), name="kbar_sample",
    )(page_table, cache_pages)


def _gate_sample_kernel(q_ref, kbar_ref, idx_ref, gate_ref):
    n_seq, _, n_blk, _ = kbar_ref.shape
    for b in range(n_seq):
        for kh in range(N_KV_HEADS):
            r0 = b * N_ATT_HEADS + kh * Q_PER_KV
            q4 = q_ref[r0:r0 + Q_PER_KV, :].astype(BF16)
            gate_ref[r0:r0 + Q_PER_KV, :] = lax.dot_general(
                q4, kbar_ref[b, kh].astype(BF16), NT_DIMS, preferred_element_type=F32)
    gate = gate_ref[...]
    lane = lax.broadcasted_iota(jnp.int32, gate.shape, 1)
    idx_ref[...] = jnp.zeros_like(idx_ref)
    for r in range(MOBA_TOPK):
        best = jnp.max(gate, axis=1, keepdims=True)
        first = jnp.min(jnp.where(gate == best, lane, n_blk), axis=1, keepdims=True)
        idx_ref[:, r:r + 1] = first
        gate = jnp.where(lane == first, -jnp.inf, gate)


def _gate_sample(q_rows, kbar):
    n_seq, _, n_blk, _ = kbar.shape
    assert n_blk >= MOBA_TOPK
    rows = n_seq * N_ATT_HEADS
    return pl.pallas_call(
        _gate_sample_kernel, out_shape=jax.ShapeDtypeStruct((rows, LANES), jnp.int32), grid=(1,),
        in_specs=[_resident((rows, HEAD_DIM)), _resident(kbar.shape)],
        out_specs=pl.BlockSpec((rows, LANES), lambda i: (0, 0)),
        scratch_shapes=[pltpu.VMEM((rows, n_blk), F32)],
        compiler_params=_cparams("arbitrary"), name="gate_sample",
    )(q_rows, kbar)


def _attn_sample_kernel(pt_ref, idx_ref, q_ref, knew_ref, vnew_ref, g_ref, ck_ref, cv_ref, o_ref,
                        kbuf_ref, vbuf_ref, sem_ref, *, layer):
    b = pl.program_id(0)
    n_sel = N_ATT_HEADS * MOBA_TOPK
    cur = b % 2

    def copies(seq, buf):
        out = []
        for s in range(n_sel):
            blk = idx_ref[seq * n_sel + s]
            for p in range(PAGES_PER_BLOCK):
                page = pt_ref[seq, blk * PAGES_PER_BLOCK + p]
                slot = s * PAGES_PER_BLOCK + p
                kh = (s // MOBA_TOPK) // Q_PER_KV
                out.append(pltpu.make_async_copy(ck_ref.at[layer, page, :, kh, :], kbuf_ref.at[buf, slot],
                                                 sem_ref.at[buf, 0]))
                out.append(pltpu.make_async_copy(cv_ref.at[layer, page, :, kh, :], vbuf_ref.at[buf, slot],
                                                 sem_ref.at[buf, 1]))
        return out

    @pl.when(b == 0)
    def _():
        for cp in copies(0, 0):
            cp.start()

    @pl.when(b + 1 < pl.num_programs(0))
    def _():
        for cp in copies(b + 1, 1 - cur):
            cp.start()

    for cp in copies(b, cur):
        cp.wait()

    pages_per_head = MOBA_TOPK * PAGES_PER_BLOCK
    heads = range(N_ATT_HEADS)
    qs = [q_ref[0, h:h + 1, :].astype(BF16) for h in heads]
    scores = []
    for h in heads:
        q8 = jnp.broadcast_to(qs[h], (SUBLANES, HEAD_DIM))
        scores.append([lax.dot_general(q8, kbuf_ref[cur, h * pages_per_head + s].astype(BF16), NT_DIMS,
                                       preferred_element_type=F32)[0:1, :] for s in range(pages_per_head)])
    probs, denom, own = [], [], []
    for h in heads:
        kh = h // Q_PER_KV
        k_self = knew_ref[0, kh:kh + 1, :].astype(BF16).astype(F32)
        s_self = jnp.sum(qs[h].astype(F32) * k_self, axis=1, keepdims=True)
        m = s_self
        for sc in scores[h]:
            m = jnp.maximum(m, jnp.max(sc, axis=1, keepdims=True))
        p_self = jnp.exp((s_self - m) * ATT_SCALE)
        ps = [jnp.exp((sc - m) * ATT_SCALE) for sc in scores[h]]
        l = p_self
        for p in ps:
            l = l + jnp.sum(p, axis=1, keepdims=True)
        probs.append([jnp.broadcast_to(p.astype(BF16), (SUBLANES, PAGE_SIZE)) for p in ps])
        denom.append(l)
        own.append(p_self.astype(BF16).astype(F32) * vnew_ref[0, kh:kh + 1, :].astype(BF16).astype(F32))
    for h in heads:
        acc = own[h]
        for s in range(pages_per_head):
            page = vbuf_ref[cur, h * pages_per_head + s].astype(BF16)
            acc = acc + jnp.dot(probs[h][s], page, preferred_element_type=F32)[0:1, :]
        o_ref[0, h:h + 1, :] = ((acc / denom[h]) * _silu(g_ref[0, h:h + 1, :])).astype(o_ref.dtype)


def _attn_sample(q3, knew3, vnew3, g3, cache_k_pages, cache_v_pages, page_table, idx_flat, layer):
    n_seq = q3.shape[0]
    n_slots = N_ATT_HEADS * MOBA_TOPK * PAGES_PER_BLOCK
    per_seq = lambda r: pl.BlockSpec((1, r, HEAD_DIM), lambda b, pt, ix: (b, 0, 0))
    return pl.pallas_call(
        functools.partial(_attn_sample_kernel, layer=layer),
        out_shape=jax.ShapeDtypeStruct((n_seq, N_ATT_HEADS, HEAD_DIM), BF16),
        grid_spec=pltpu.PrefetchScalarGridSpec(
            num_scalar_prefetch=2, grid=(n_seq,),
            in_specs=[per_seq(N_ATT_HEADS), per_seq(N_KV_HEADS), per_seq(N_KV_HEADS), per_seq(N_ATT_HEADS),
                      pl.BlockSpec(memory_space=pl.ANY), pl.BlockSpec(memory_space=pl.ANY)],
            out_specs=per_seq(N_ATT_HEADS),
            scratch_shapes=[pltpu.VMEM((2, n_slots, PAGE_SIZE, HEAD_DIM), F32),
                            pltpu.VMEM((2, n_slots, PAGE_SIZE, HEAD_DIM), F32),
                            pltpu.SemaphoreType.DMA((2, 2))]),
        compiler_params=_cparams("arbitrary"), name="attn_sample",
    )(page_table, idx_flat, q3, knew3, vnew3, g3, cache_k_pages, cache_v_pages)


def _column(row):
    return jnp.broadcast_to(row, (LANES, LANES)).T


def _ssd_sample_kernel(xbc_ref, z_ref, dt_ref, sc_ref, ss_ref, convw_ref, convb_ref, dtb_ref, acoef_ref,
                       dskip_ref, ng_ref, o_ref, nc_ref, ns_ref, y_ref):
    hist = sc_ref[0]
    new = xbc_ref[0]
    conv = convb_ref[...] + convw_ref[CONV_WIDTH - 1:CONV_WIDTH, :] * new
    for w in range(CONV_WIDTH - 1):
        conv = conv + convw_ref[w:w + 1, :] * hist[w:w + 1, :]
    nc_ref[0, :CONV_WIDTH - 2, :] = hist[1:, :]
    nc_ref[0, CONV_WIDTH - 2:, :] = new
    xc = _silu(conv)
    xs = xc[:, :SSD_WIDTH]
    dt = jax.nn.softplus(dt_ref[0] + dtb_ref[...])
    d_a = jnp.exp(dt * acoef_ref[...])
    xdt = xs * dt
    heads_per_chunk = LANES // SSD_HEAD_DIM
    chunks = range(SSD_WIDTH // LANES)
    x_cols = [_column(xdt[:, c * LANES:(c + 1) * LANES]) for c in chunks]
    da_cols = [_column(d_a[:, c * LANES:(c + 1) * LANES]) for c in chunks]
    y_cols = []
    for c in chunks:
        parts = []
        for hh in range(heads_per_chunk):
            h = c * heads_per_chunk + hh
            g = h // HEADS_PER_GROUP
            rows = slice(hh * SSD_HEAD_DIM, (hh + 1) * SSD_HEAD_DIM)
            b_g = xc[:, SSD_WIDTH + g * D_STATE:SSD_WIDTH + (g + 1) * D_STATE]
            c_g = xc[:, SSD_WIDTH + (SSD_GROUPS + g) * D_STATE:SSD_WIDTH + (SSD_GROUPS + g + 1) * D_STATE]
            st = ss_ref[0, h] * da_cols[c][rows, :] + x_cols[c][rows, :] * b_g
            ns_ref[0, h] = st
            parts.append(jnp.sum(st * c_g, axis=1, keepdims=True))
        y_cols.append(jnp.concatenate(parts, axis=0))
    for c in chunks:
        y_ref[:, c * LANES:(c + 1) * LANES] = jnp.broadcast_to(y_cols[c], (LANES, LANES)).T[0:1, :]
    y = (y_ref[...] + dskip_ref[...] * xs) * _silu(z_ref[0])
    o_ref[0] = _rmsnorm(y, ng_ref[...]).astype(o_ref.dtype)


def _ssd_sample(xbc3, z3, dte3, state_conv, state_ssm, layer, conv_w, conv_b, dt_bias, a_log, d_skip, ssd_norm_g):
    n_seq = xbc3.shape[0]
    per_channel = lambda v: jnp.repeat(v.astype(F32), SSD_HEAD_DIM).reshape(1, SSD_WIDTH)
    seq3 = lambda w: pl.BlockSpec((1, 1, w), lambda b: (b, 0, 0))
    state_spec = pl.BlockSpec((None, 1, N_SSD_HEADS, SSD_HEAD_DIM, D_STATE), lambda b: (layer, b, 0, 0, 0))
    in_specs = [seq3(CONV_DIM), seq3(SSD_WIDTH), seq3(SSD_WIDTH),
                pl.BlockSpec((None, 1, CONV_WIDTH - 1, CONV_DIM), lambda b: (layer, b, 0, 0)), state_spec,
                _resident((CONV_WIDTH, CONV_DIM)), _resident((1, CONV_DIM)), _resident((1, SSD_WIDTH)),
                _resident((1, SSD_WIDTH)), _resident((1, SSD_WIDTH)), _resident((1, SSD_WIDTH))]
    operands = [xbc3, z3, dte3, state_conv, state_ssm, conv_w, conv_b.reshape(1, CONV_DIM), per_channel(dt_bias),
                per_channel(-jnp.exp(a_log.astype(F32))), per_channel(d_skip), ssd_norm_g.reshape(1, SSD_WIDTH)]
    return pl.pallas_call(
        _ssd_sample_kernel,
        out_shape=(jax.ShapeDtypeStruct((n_seq, 1, SSD_WIDTH), BF16),
                   jax.ShapeDtypeStruct((n_seq, CONV_WIDTH - 1, CONV_DIM), F32),
                   jax.ShapeDtypeStruct((n_seq, N_SSD_HEADS, SSD_HEAD_DIM, D_STATE), F32)),
        grid=(n_seq,), in_specs=in_specs,
        out_specs=(seq3(SSD_WIDTH), pl.BlockSpec((1, CONV_WIDTH - 1, CONV_DIM), lambda b: (b, 0, 0)),
                   pl.BlockSpec((1, N_SSD_HEADS, SSD_HEAD_DIM, D_STATE), lambda b: (b, 0, 0, 0))),
        scratch_shapes=[pltpu.VMEM((1, SSD_WIDTH), F32)],
        compiler_params=_cparams("parallel"), name="ssd_sample",
    )(*operands)


def _sample_layer(x, layer, cache_k, cache_v, state_conv, state_ssm, page_table,
                  norm_g, w_in, conv_w, conv_b, dt_bias, a_log, d_skip, ssd_norm_g, w_out, final_g, final_norm):
    n_seq = x.shape[0]
    u = _in_proj_sample(x, norm_g, w_in)
    q, k, v, g_att = u[:, :Q_END], u[:, Q_END:K_END], u[:, K_END:V_END], u[:, V_END:G_END]
    z, xbc, dte = u[:, G_END:Z_END], u[:, Z_END:XBC_END], u[:, XBC_END:]
    cache_k_pages = cache_k.reshape(cache_k.shape[0], cache_k.shape[1], PAGE_ROWS, HEAD_DIM)
    kbar = _kbar_sample(cache_k_pages, page_table, layer)
    idx = _gate_sample(q.reshape(n_seq * N_ATT_HEADS, HEAD_DIM), kbar)
    heads = lambda a, r: a.reshape(n_seq, r, HEAD_DIM)
    o_att = _attn_sample(heads(q, N_ATT_HEADS), heads(k, N_KV_HEADS), heads(v, N_KV_HEADS),
                         heads(g_att, N_ATT_HEADS), cache_k, cache_v, page_table,
                         idx[:, :MOBA_TOPK].reshape(-1), layer)
    rows3 = lambda a: a.reshape(n_seq, 1, a.shape[1])
    o_ssd, new_conv, new_ssm = _ssd_sample(rows3(xbc), rows3(z), rows3(dte), state_conv, state_ssm, layer,
                                           conv_w, conv_b, dt_bias, a_log, d_skip, ssd_norm_g)
    h = _out_proj(x, o_att.reshape(n_seq, ATT_WIDTH), o_ssd.reshape(n_seq, SSD_WIDTH), w_out.astype(BF16),
                  final_g, final_norm, n_seq)
    return h, k, v, new_conv, new_ssm


PROMPT_ROW_TILE = 512


def kernel(x_prompt, x_sample, cache_k, cache_v, state_conv, state_ssm, page_table, norm_g, w_in, conv_w,
           conv_b, dt_bias, a_log, d_skip, ssd_norm_g, w_out, final_norm_g):
    depth = w_in.shape[0]
    batch, seq, _ = x_prompt.shape
    n_seq, dec_seq, _ = x_sample.shape
    assert batch == 1 and dec_seq == 1
    hp, hs = x_prompt[0], x_sample[:, 0]
    outs = [[] for _ in range(8)]
    for l in range(depth):
        last = l == depth - 1
        params = (norm_g[l], _in_proj_weights(w_in[l]), conv_w[l], conv_b[l], dt_bias[l], a_log[l], d_skip[l],
                  ssd_norm_g[l], w_out[l], final_norm_g, last)
        hp, kp, vp, cp, sp = _prompt_layer(hp, *params, PROMPT_ROW_TILE)
        hs, ks, vs, cs, ss = _sample_layer(hs, l, cache_k, cache_v, state_conv, state_ssm, page_table, *params)
        kv4 = lambda a: a.reshape(n_seq, dec_seq, N_KV_HEADS, HEAD_DIM)
        for lst, val in zip(outs, (kp[None], vp[None], cp[None], sp[None], kv4(ks), kv4(vs), cs, ss)):
            lst.append(val)
    return (hp[None], hs[:, None]) + tuple(jnp.stack(o) for o in outs)
```

```python
import functools

import jax
import jax.numpy as jnp
from jax import lax
from jax.experimental import pallas as pl
from jax.experimental.pallas import tpu as pltpu

F32 = jnp.float32
BF16 = jnp.bfloat16

D_MODEL = 1024
N_ATT_HEADS = 8
N_KV_HEADS = 2
HEAD_DIM = 128
Q_PER_KV = N_ATT_HEADS // N_KV_HEADS
ATT_WIDTH = N_ATT_HEADS * HEAD_DIM
KV_WIDTH = N_KV_HEADS * HEAD_DIM
MOBA_BLOCK = 256
MOBA_TOPK = 3
PAGE_SIZE = 128
SSD_WIDTH = 1024
SSD_HEAD_DIM = 64
N_SSD_HEADS = SSD_WIDTH // SSD_HEAD_DIM
SSD_GROUPS = 2
HEADS_PER_GROUP = N_SSD_HEADS // SSD_GROUPS
GROUP_WIDTH = HEADS_PER_GROUP * SSD_HEAD_DIM
D_STATE = 128
CONV_WIDTH = 4
SSD_CHUNK = 256
CONV_DIM = SSD_WIDTH + 2 * SSD_GROUPS * D_STATE
MIX_WIDTH = ATT_WIDTH + SSD_WIDTH
Q_END = ATT_WIDTH
K_END = Q_END + KV_WIDTH
V_END = K_END + KV_WIDTH
G_END = V_END + ATT_WIDTH
Z_END = G_END + SSD_WIDTH
XBC_END = Z_END + CONV_DIM
IN_WIDTH = XBC_END + N_SSD_HEADS
NORM_EPS = 1e-5
ATT_SCALE = HEAD_DIM ** -0.5
EXP2_SCALE = ATT_SCALE * 1.4426950408889634

LANES = 128
SUBLANES = 8
VMEM_LIMIT_BYTES = 56 * 1024 * 1024

MASK_BIAS = -1e30
VT_ONES = 16
VT_ROWS = HEAD_DIM + VT_ONES
LOGITS_LEAD = 1
KEY_BLOCKS_PER_STEP = 2

NT_DIMS = (((1,), (1,)), ((), ()))
TN_DIMS = (((0,), (0,)), ((), ()))


def _cparams(*semantics):
    return pltpu.CompilerParams(dimension_semantics=semantics, vmem_limit_bytes=VMEM_LIMIT_BYTES)


def _resident(shape):
    nd = len(shape)
    return pl.BlockSpec(shape, lambda *_: (0,) * nd, pipeline_mode=pl.Buffered(1))


def _rmsnorm(x, g):
    return x * lax.rsqrt(jnp.mean(x * x, axis=-1, keepdims=True) + NORM_EPS) * g


def _silu(x):
    return x * jax.nn.sigmoid(x)


def _reduce_rows(x, op, final):
    slabs = [x[r:r + SUBLANES] for r in range(0, x.shape[0], SUBLANES)]
    while len(slabs) > 1:
        nxt = [op(slabs[a], slabs[a + 1]) for a in range(0, len(slabs) - 1, 2)]
        if len(slabs) % 2:
            nxt.append(slabs[-1])
        slabs = nxt
    return final(slabs[0], axis=0, keepdims=True)


def _split3(x):
    hi = x.astype(BF16)
    r = x - hi.astype(F32)
    mid = r.astype(BF16)
    lo = (r - mid.astype(F32)).astype(BF16)
    return hi, mid, lo


def _dot3(parts, m, left=False):
    if left:
        f = lambda p: jnp.dot(m, p, preferred_element_type=F32)
    else:
        f = lambda p: jnp.dot(p, m, preferred_element_type=F32)
    return (f(parts[2]) + f(parts[1])) + f(parts[0])


def _in_proj_kernel(x_ref, g_ref, wqT_ref, wkv_ref, wvT_ref, wrest_ref, wdt_ref,
                    qT_ref, k_ref, v_ref, ka_ref, vT_ref, gz_ref, xbc_ref, dt_ref, ksum_ref):
    tm = x_ref.shape[0]
    xn = _rmsnorm(x_ref[...], g_ref[...]).astype(BF16)
    qT = lax.dot_general(wqT_ref[...], xn, NT_DIMS, preferred_element_type=F32)
    qT_ref[...] = (qT * EXP2_SCALE).astype(BF16)
    vT = lax.dot_general(wvT_ref[...], xn, NT_DIMS, preferred_element_type=F32).astype(BF16)
    kv = jnp.dot(xn, wkv_ref[...], preferred_element_type=F32)
    k = kv[:, :KV_WIDTH]
    for h in range(N_KV_HEADS):
        k_ref[:, h, :] = kv[:, h * HEAD_DIM:(h + 1) * HEAD_DIM]
        v_ref[:, h, :] = kv[:, KV_WIDTH + h * HEAD_DIM:KV_WIDTH + (h + 1) * HEAD_DIM]
    blk0 = pl.program_id(0) * (tm // MOBA_BLOCK)
    lane = lax.broadcasted_iota(jnp.int32, (MOBA_BLOCK, LANES), 1)
    for b in range(tm // MOBA_BLOCK):
        rows = slice(b * MOBA_BLOCK, (b + 1) * MOBA_BLOCK)
        onehot = (lane == blk0 + b).astype(BF16)
        for h in range(N_KV_HEADS):
            ka_ref[h, b, :, :HEAD_DIM] = k[rows, h * HEAD_DIM:(h + 1) * HEAD_DIM].astype(BF16)
            ka_ref[h, b, :, HEAD_DIM:] = onehot
        for h in range(N_KV_HEADS):
            vT_ref[b, h * VT_ROWS:h * VT_ROWS + HEAD_DIM, :] = vT[h * HEAD_DIM:(h + 1) * HEAD_DIM, rows]
            vT_ref[b, h * VT_ROWS + HEAD_DIM:(h + 1) * VT_ROWS, :] = jnp.ones((VT_ONES, MOBA_BLOCK), BF16)
        ksum_ref[0, b:b + 1, :] = jnp.sum(k[rows], axis=0, keepdims=True)
    rest = jnp.dot(xn, wrest_ref[...], preferred_element_type=F32)
    gz_ref[...] = rest[:, :ATT_WIDTH + SSD_WIDTH]
    xbc_ref[...] = rest[:, ATT_WIDTH + SSD_WIDTH:]
    dt_ref[...] = jnp.dot(xn, wdt_ref[...], preferred_element_type=F32)


def _in_proj_weights(w_in):
    wb = w_in.astype(BF16)
    wqT = wb[:, :Q_END].T
    wkv = wb[:, Q_END:V_END]
    wvT = wb[:, K_END:V_END].T
    wrest = wb[:, V_END:XBC_END]
    wdt = jnp.pad(wb[:, XBC_END:], ((0, 0), (0, LANES - N_SSD_HEADS)))
    return wqT, wkv, wvT, wrest, wdt


def _in_proj(x, norm_g, weights, tm):
    n = x.shape[0]
    assert n % tm == 0 and tm % MOBA_BLOCK == 0
    wqT, wkv, wvT, wrest, wdt = weights
    nb = tm // MOBA_BLOCK
    row = lambda i: (i, 0)
    kv_spec = pl.BlockSpec((tm, N_KV_HEADS, HEAD_DIM), lambda i: (i, 0, 0))
    out_shape = (
        jax.ShapeDtypeStruct((ATT_WIDTH, n), BF16),
        jax.ShapeDtypeStruct((n, N_KV_HEADS, HEAD_DIM), F32),
        jax.ShapeDtypeStruct((n, N_KV_HEADS, HEAD_DIM), F32),
        jax.ShapeDtypeStruct((N_KV_HEADS, n // MOBA_BLOCK, MOBA_BLOCK, 2 * HEAD_DIM), BF16),
        jax.ShapeDtypeStruct((n // MOBA_BLOCK, N_KV_HEADS * VT_ROWS, MOBA_BLOCK), BF16),
        jax.ShapeDtypeStruct((n, ATT_WIDTH + SSD_WIDTH), F32),
        jax.ShapeDtypeStruct((n, CONV_DIM), F32),
        jax.ShapeDtypeStruct((n, LANES), F32),
        jax.ShapeDtypeStruct((n // tm, nb, KV_WIDTH), F32),
    )
    out_specs = (
        pl.BlockSpec((ATT_WIDTH, tm), lambda i: (0, i)),
        kv_spec,
        kv_spec,
        pl.BlockSpec((N_KV_HEADS, nb, MOBA_BLOCK, 2 * HEAD_DIM), lambda i: (0, i, 0, 0)),
        pl.BlockSpec((nb, N_KV_HEADS * VT_ROWS, MOBA_BLOCK), lambda i: (i, 0, 0)),
        pl.BlockSpec((tm, ATT_WIDTH + SSD_WIDTH), row),
        pl.BlockSpec((tm, CONV_DIM), row),
        pl.BlockSpec((tm, LANES), row),
        pl.BlockSpec((1, nb, KV_WIDTH), lambda i: (i, 0, 0)),
    )
    in_specs = [pl.BlockSpec((tm, D_MODEL), row), _resident((1, D_MODEL)), _resident(wqT.shape),
                _resident(wkv.shape), _resident(wvT.shape), _resident(wrest.shape), _resident(wdt.shape)]
    return pl.pallas_call(
        _in_proj_kernel, out_shape=out_shape, grid=(n // tm,), in_specs=in_specs, out_specs=out_specs,
        compiler_params=_cparams("parallel"), name="in_proj",
    )(x, norm_g.reshape(1, D_MODEL), wqT, wkv, wvT, wrest, wdt)


def _moba_prompt_kernel(qT_ref, ka_ref, vT_ref, ksum_ref, g_ref, o_ref, qa_ref, m_ref, acc_ref,
                        s0_ref, c0_ref, s1_ref, c1_ref):
    i = pl.program_id(1)
    n_blk = ksum_ref.shape[0]
    kbar = (ksum_ref[...] * (1.0 / MOBA_BLOCK)).astype(BF16)
    blk = lax.broadcasted_iota(jnp.int32, (n_blk, MOBA_BLOCK), 0)
    key_pos = lax.broadcasted_iota(jnp.int32, (MOBA_BLOCK, MOBA_BLOCK), 0)
    qry_pos = lax.broadcasted_iota(jnp.int32, (MOBA_BLOCK, MOBA_BLOCK), 1)

    for h in range(Q_PER_KV):
        cols = slice(h * MOBA_BLOCK, (h + 1) * MOBA_BLOCK)
        qT = qT_ref[h * HEAD_DIM:(h + 1) * HEAD_DIM, :]
        gate = jnp.dot(kbar, qT, preferred_element_type=F32)
        gate = jnp.where(blk < i, gate, -jnp.inf)
        bias = jnp.full((n_blk, MOBA_BLOCK), MASK_BIAS, F32)
        for r in range(MOBA_TOPK):
            best = jnp.max(gate, axis=0, keepdims=True)
            first = jnp.min(jnp.where(gate == best, blk, n_blk), axis=0, keepdims=True)
            first = jnp.where(r < i, first, -1)
            pick = blk == first
            bias = jnp.where(pick, 0.0, bias)
            gate = jnp.where(pick, -jnp.inf, gate)
        qa_ref[:HEAD_DIM, cols] = qT
        qa_ref[HEAD_DIM:HEAD_DIM + n_blk, cols] = bias.astype(BF16)
        if n_blk < LANES:
            qa_ref[HEAD_DIM + n_blk:, cols] = jnp.zeros((LANES - n_blk, MOBA_BLOCK), BF16)

    def own_block():
        k_own = ka_ref[i][:, :HEAD_DIM]
        logit = [jnp.dot(k_own, qT_ref[h * HEAD_DIM:(h + 1) * HEAD_DIM, :], preferred_element_type=F32)
                 for h in range(Q_PER_KV)]
        probs = []
        for h in range(Q_PER_KV):
            s = jnp.where(key_pos <= qry_pos, logit[h], MASK_BIAS)
            m_new = _reduce_rows(s, jnp.maximum, jnp.max)
            m_ref[h] = m_new
            probs.append(jnp.exp2(s - m_new).astype(BF16))
        for h in range(Q_PER_KV):
            acc_ref[h] = jnp.dot(vT_ref[i], probs[h], preferred_element_type=F32)

    kb = KEY_BLOCKS_PER_STEP
    n_groups = (i + kb - 1) // kb
    last_group = n_groups - 1

    def keys_of(group):
        return ka_ref[pl.ds(kb * group, kb)].reshape(kb * MOBA_BLOCK, 2 * HEAD_DIM)

    def head_logits(k_aug, h, dst_ref, dmax_ref):
        s = jnp.dot(k_aug, qa_ref[:, h * MOBA_BLOCK:(h + 1) * MOBA_BLOCK], preferred_element_type=F32)
        dst_ref[h] = s
        dmax_ref[h] = _reduce_rows(s, jnp.maximum, jnp.max)

    def logits(group, dst_ref, dmax_ref):
        k_aug = keys_of(group)
        for h in range(Q_PER_KV):
            head_logits(k_aug, h, dst_ref, dmax_ref)

    def softmax_step(group, cur_ref, cmax_ref, nxt_ref, nmax_ref):
        k_next = keys_of(jnp.minimum(group + 1, last_group))
        vT = jnp.concatenate([vT_ref[kb * group + b] for b in range(kb)], axis=1)
        for h in range(LOGITS_LEAD):
            head_logits(k_next, h, nxt_ref, nmax_ref)
        for h in range(Q_PER_KV):
            if h + LOGITS_LEAD < Q_PER_KV:
                head_logits(k_next, h + LOGITS_LEAD, nxt_ref, nmax_ref)
            m_old = m_ref[h]
            m_new = jnp.maximum(m_old, cmax_ref[h])
            p = jnp.exp2(cur_ref[h] - m_new).astype(BF16)
            acc_ref[h] = jnp.exp2(m_old - m_new) * acc_ref[h] + jnp.dot(vT, p, preferred_element_type=F32)
            m_ref[h] = m_new

    own_block()
    logits(0, s0_ref, c0_ref)

    def one_group(group, carry):
        @pl.when(group % 2 == 0)
        def _():
            softmax_step(group, s0_ref, c0_ref, s1_ref, c1_ref)

        @pl.when(group % 2 == 1)
        def _():
            softmax_step(group, s1_ref, c1_ref, s0_ref, c0_ref)

        return carry

    lax.fori_loop(0, n_groups, one_group, 0)

    for h in range(Q_PER_KV):
        acc = acc_ref[h]
        out = (acc[:HEAD_DIM] / acc[HEAD_DIM:HEAD_DIM + 1]).T
        dcols = slice(h * HEAD_DIM, (h + 1) * HEAD_DIM)
        o_ref[:, dcols] = (out * _silu(g_ref[:, dcols])).astype(BF16)


def _moba_prompt(qT, ka, vT, ksum, gz):
    n = qT.shape[1]
    n_blk = n // MOBA_BLOCK
    assert n_blk <= LANES and n_blk % KEY_BLOCKS_PER_STEP == 0
    kv_rows = Q_PER_KV * HEAD_DIM
    logit_buf = pltpu.VMEM((Q_PER_KV, KEY_BLOCKS_PER_STEP * MOBA_BLOCK, MOBA_BLOCK), F32)
    stat_buf = pltpu.VMEM((Q_PER_KV, 1, MOBA_BLOCK), F32)
    return pl.pallas_call(
        _moba_prompt_kernel,
        out_shape=jax.ShapeDtypeStruct((n, ATT_WIDTH), BF16),
        grid=(N_KV_HEADS, n_blk),
        in_specs=[
            pl.BlockSpec((kv_rows, MOBA_BLOCK), lambda h, i: (h, i)),
            pl.BlockSpec((None, n_blk, MOBA_BLOCK, 2 * HEAD_DIM), lambda h, i: (h, 0, 0, 0)),
            pl.BlockSpec((n_blk, VT_ROWS, MOBA_BLOCK), lambda h, i: (0, h, 0)),
            pl.BlockSpec((n_blk, HEAD_DIM), lambda h, i: (0, h)),
            pl.BlockSpec((MOBA_BLOCK, kv_rows), lambda h, i: (i, h)),
        ],
        out_specs=pl.BlockSpec((MOBA_BLOCK, kv_rows), lambda h, i: (i, h)),
        scratch_shapes=[pltpu.VMEM((2 * HEAD_DIM, Q_PER_KV * MOBA_BLOCK), BF16),
                        stat_buf,
                        pltpu.VMEM((Q_PER_KV, VT_ROWS, MOBA_BLOCK), F32),
                        logit_buf, stat_buf, logit_buf, stat_buf],
        compiler_params=_cparams("parallel", "parallel"), name="moba_prompt",
    )(qT, ka, vT, ksum, gz)


def _ssd_prompt_kernel(xbc_ref, z_ref, dt_ref, convw_ref, convb_ref, dtb_ref, acoef_ref, dskipe_ref,
                       ng_ref, o_ref, state_ref, xpad_ref, st_ref, y_ref):
    c = pl.program_id(0)
    t = SSD_CHUNK

    @pl.when(c == 0)
    def _():
        xpad_ref[:SUBLANES, :] = jnp.zeros((SUBLANES, CONV_DIM), F32)
        st_ref[...] = jnp.zeros_like(st_ref)

    xpad_ref[SUBLANES:, :] = xbc_ref[...]
    conv = convb_ref[...] + convw_ref[CONV_WIDTH - 1:CONV_WIDTH, :] * xpad_ref[SUBLANES:, :]
    for w in range(CONV_WIDTH - 1):
        back = CONV_WIDTH - 1 - w
        conv = conv + convw_ref[w:w + 1, :] * xpad_ref[SUBLANES - back:SUBLANES - back + t, :]
    xpad_ref[:SUBLANES, :] = xpad_ref[t:t + SUBLANES, :]
    xc = _silu(conv)
    xs = xc[:, :SSD_WIDTH]

    dt = jax.nn.softplus(dt_ref[...] + dtb_ref[...])
    head_of_channel = lax.broadcasted_iota(jnp.int32, (LANES, SSD_WIDTH), 1) // SSD_HEAD_DIM
    expand = (head_of_channel == lax.broadcasted_iota(jnp.int32, (LANES, SSD_WIDTH), 0)).astype(BF16)
    dt_e = _dot3(_split3(dt), expand)
    lower = lax.broadcasted_iota(jnp.int32, (t, t), 0) >= lax.broadcasted_iota(jnp.int32, (t, t), 1)
    tril = lower.astype(BF16)
    cumsum = lambda v: _dot3(_split3(v), tril, left=True)
    a_cum = cumsum(dt * acoef_ref[...])
    a_cum_t = a_cum.T
    a_cum_e = _dot3(_split3(a_cum), expand)
    a_end_e = a_cum_e[t - 1:t, :]

    xdt = xs * dt_e
    xdt_end = (xdt * jnp.exp(a_end_e - a_cum_e)).astype(BF16)
    from_start = jnp.exp(a_cum_e)
    state_decay = jnp.exp(a_end_e)
    lane = lax.broadcasted_iota(jnp.int32, (t, LANES), 1)

    for g in range(SSD_GROUPS):
        gc = slice(g * GROUP_WIDTH, (g + 1) * GROUP_WIDTH)
        b_bf = xc[:, SSD_WIDTH + g * D_STATE:SSD_WIDTH + (g + 1) * D_STATE].astype(BF16)
        c_bf = xc[:, SSD_WIDTH + (SSD_GROUPS + g) * D_STATE:
                  SSD_WIDTH + (SSD_GROUPS + g + 1) * D_STATE].astype(BF16)
        cb = lax.dot_general(c_bf, b_bf, NT_DIMS, preferred_element_type=F32)
        st = st_ref[g]
        y_ref[:, gc] = jnp.dot(c_bf, st.astype(BF16), preferred_element_type=F32) * from_start[:, gc]
        st_ref[g] = st * state_decay[:, gc] + lax.dot_general(b_bf, xdt_end[:, gc], TN_DIMS,
                                                               preferred_element_type=F32)
        for pair in range(GROUP_WIDTH // LANES):
            pc = slice(g * GROUP_WIDTH + pair * LANES, g * GROUP_WIDTH + (pair + 1) * LANES)
            xdt_pair = xdt[:, pc]
            y_pair = y_ref[:, pc] + dskipe_ref[:, pc] * xs[:, pc]
            for hh in range(LANES // SSD_HEAD_DIM):
                h = (g * GROUP_WIDTH + pair * LANES) // SSD_HEAD_DIM + hh
                decay = jnp.exp(jnp.where(lower, a_cum[:, h:h + 1] - a_cum_t[h:h + 1, :], -jnp.inf))
                mine = (lane // SSD_HEAD_DIM) == hh
                rhs = jnp.where(mine, xdt_pair, 0.0).astype(BF16)
                y_pair = y_pair + jnp.dot((cb * decay).astype(BF16), rhs, preferred_element_type=F32)
            y_ref[:, pc] = y_pair

    y = y_ref[...] * _silu(z_ref[...])
    o_ref[...] = _rmsnorm(y, ng_ref[...]).astype(o_ref.dtype)

    @pl.when(c == pl.num_programs(0) - 1)
    def _():
        for g in range(SSD_GROUPS):
            for pair in range(GROUP_WIDTH // LANES):
                both = st_ref[g][:, pair * LANES:(pair + 1) * LANES].T
                for hh in range(LANES // SSD_HEAD_DIM):
                    h = (g * GROUP_WIDTH + pair * LANES) // SSD_HEAD_DIM + hh
                    state_ref[h] = both[hh * SSD_HEAD_DIM:(hh + 1) * SSD_HEAD_DIM, :]


def _ssd_prompt(xbc, gz, dt_raw, conv_w, conv_b, dt_bias, a_log, d_skip, ssd_norm_g):
    n = xbc.shape[0]
    t = SSD_CHUNK
    assert n % t == 0
    pad = lambda v: jnp.pad(v.reshape(1, N_SSD_HEADS), ((0, 0), (0, LANES - N_SSD_HEADS)))
    per_channel = lambda v: jnp.repeat(v.astype(F32), SSD_HEAD_DIM).reshape(1, SSD_WIDTH)
    a_coef = -jnp.exp(a_log.astype(F32))
    row = lambda c: (c, 0)
    return pl.pallas_call(
        _ssd_prompt_kernel,
        out_shape=(jax.ShapeDtypeStruct((n, SSD_WIDTH), BF16),
                   jax.ShapeDtypeStruct((N_SSD_HEADS, SSD_HEAD_DIM, D_STATE), F32)),
        grid=(n // t,),
        in_specs=[pl.BlockSpec((t, CONV_DIM), row),
                  pl.BlockSpec((t, SSD_WIDTH), lambda c: (c, 1)),
                  pl.BlockSpec((t, LANES), row),
                  _resident((CONV_WIDTH, CONV_DIM)), _resident((1, CONV_DIM)), _resident((1, LANES)),
                  _resident((1, LANES)), _resident((1, SSD_WIDTH)), _resident((1, SSD_WIDTH))],
        out_specs=(pl.BlockSpec((t, SSD_WIDTH), row),
                   pl.BlockSpec((N_SSD_HEADS, SSD_HEAD_DIM, D_STATE), lambda c: (0, 0, 0))),
        scratch_shapes=[pltpu.VMEM((t + SUBLANES, CONV_DIM), F32),
                        pltpu.VMEM((SSD_GROUPS, D_STATE, GROUP_WIDTH), F32),
                        pltpu.VMEM((t, SSD_WIDTH), F32)],
        compiler_params=_cparams("arbitrary"), name="ssd_prompt",
    )(xbc, gz, dt_raw, conv_w, conv_b.reshape(1, CONV_DIM), pad(dt_bias), pad(a_coef), per_channel(d_skip),
      ssd_norm_g.reshape(1, SSD_WIDTH))


def _out_proj_kernel(x_ref, oa_ref, os_ref, w_ref, fg_ref, h_ref, *, final_norm):
    out = jnp.dot(oa_ref[...], w_ref[:ATT_WIDTH, :], preferred_element_type=F32)
    out = out + jnp.dot(os_ref[...], w_ref[ATT_WIDTH:, :], preferred_element_type=F32)
    h = x_ref[...] + out
    h_ref[...] = _rmsnorm(h, fg_ref[...]) if final_norm else h


def _out_proj(x, o_att, o_ssd, w_out_bf, final_g, final_norm, tm):
    n = x.shape[0]
    assert n % tm == 0
    row = lambda i: (i, 0)
    return pl.pallas_call(
        functools.partial(_out_proj_kernel, final_norm=final_norm),
        out_shape=jax.ShapeDtypeStruct((n, D_MODEL), F32),
        grid=(n // tm,),
        in_specs=[pl.BlockSpec((tm, D_MODEL), row), pl.BlockSpec((tm, ATT_WIDTH), row),
                  pl.BlockSpec((tm, SSD_WIDTH), row), _resident((MIX_WIDTH, D_MODEL)), _resident((1, D_MODEL))],
        out_specs=pl.BlockSpec((tm, D_MODEL), row),
        compiler_params=_cparams("parallel"), name="out_proj",
    )(x, o_att, o_ssd, w_out_bf, final_g.reshape(1, D_MODEL))


def _prompt_layer(x, norm_g, w_in, conv_w, conv_b, dt_bias, a_log, d_skip, ssd_norm_g, w_out, final_g,
                  final_norm, tm):
    n = x.shape[0]
    qT, k, v, ka, vT, gz, xbc, dt_raw, ksum = _in_proj(x, norm_g, w_in, tm)
    o_att = _moba_prompt(qT, ka, vT, ksum.reshape(n // MOBA_BLOCK, KV_WIDTH), gz)
    o_ssd, state = _ssd_prompt(xbc, gz, dt_raw, conv_w, conv_b, dt_bias, a_log, d_skip, ssd_norm_g)
    h = _out_proj(x, o_att, o_ssd, w_out.astype(BF16), final_g, final_norm, tm)
    return h, k, v, xbc[n - (CONV_WIDTH - 1):], state


PAGES_PER_BLOCK = MOBA_BLOCK // PAGE_SIZE
PAGE_ROWS = PAGE_SIZE * N_KV_HEADS
SAMPLE_U_WIDTH = XBC_END + SSD_WIDTH


def _in_proj_sample_kernel(x_ref, g_ref, wqT_ref, wkv_ref, wrest_ref, wdte_ref, u_ref):
    xn = _rmsnorm(x_ref[...], g_ref[...]).astype(BF16)
    u_ref[:, :Q_END] = lax.dot_general(xn, wqT_ref[...], NT_DIMS, preferred_element_type=F32)
    u_ref[:, Q_END:V_END] = jnp.dot(xn, wkv_ref[...], preferred_element_type=F32)
    u_ref[:, V_END:XBC_END] = jnp.dot(xn, wrest_ref[...], preferred_element_type=F32)
    u_ref[:, XBC_END:] = jnp.dot(xn, wdte_ref[...], preferred_element_type=F32)


def _in_proj_sample(x, norm_g, weights):
    n = x.shape[0]
    wqT, wkv, _, wrest, wdt = weights
    wdte = jnp.repeat(wdt[:, :N_SSD_HEADS], SSD_HEAD_DIM, axis=1)
    return pl.pallas_call(
        _in_proj_sample_kernel, out_shape=jax.ShapeDtypeStruct((n, SAMPLE_U_WIDTH), F32), grid=(1,),
        in_specs=[_resident((n, D_MODEL)), _resident((1, D_MODEL)), _resident(wqT.shape), _resident(wkv.shape),
                  _resident(wrest.shape), _resident(wdte.shape)],
        out_specs=pl.BlockSpec((n, SAMPLE_U_WIDTH), lambda i: (0, 0)),
        compiler_params=_cparams("arbitrary"), name="in_proj_sample",
    )(x, norm_g.reshape(1, D_MODEL), wqT, wkv, wrest, wdte)


def _kbar_sample_kernel(pt_ref, cache_ref, out_ref, buf_ref, sem_ref, *, layer):
    b = pl.program_id(0)
    n_pages = pt_ref.shape[1]
    slot = b % 2

    def copies(seq, slot):
        return [pltpu.make_async_copy(cache_ref.at[layer, pt_ref[seq, p]], buf_ref.at[slot, p], sem_ref.at[slot])
                for p in range(n_pages)]

    @pl.when(b == 0)
    def _():
        for cp in copies(0, 0):
            cp.start()

    @pl.when(b + 1 < pl.num_programs(0))
    def _():
        for cp in copies(b + 1, 1 - slot):
            cp.start()

    for cp in copies(b, slot):
        cp.wait()
    for blk in range(n_pages // PAGES_PER_BLOCK):
        acc = jnp.zeros((SUBLANES, HEAD_DIM), F32)
        for p in range(PAGES_PER_BLOCK):
            page = buf_ref[slot, blk * PAGES_PER_BLOCK + p]
            acc = acc + jnp.sum(page.reshape(PAGE_ROWS // SUBLANES, SUBLANES, HEAD_DIM), axis=0)
        acc = acc + pltpu.roll(acc, 2, 0)
        acc = acc + pltpu.roll(acc, 4, 0)
        for h in range(N_KV_HEADS):
            out_ref[0, h, blk:blk + 1, :] = acc[h:h + 1, :] * (1.0 / MOBA_BLOCK)


def _kbar_sample(cache_pages, page_table, layer):
    n_seq, n_pages = page_table.shape
    assert n_pages % PAGES_PER_BLOCK == 0
    n_blk = n_pages // PAGES_PER_BLOCK
    return pl.pallas_call(
        functools.partial(_kbar_sample_kernel, layer=layer),
        out_shape=jax.ShapeDtypeStruct((n_seq, N_KV_HEADS, n_blk, HEAD_DIM), F32),
        grid_spec=pltpu.PrefetchScalarGridSpec(
            num_scalar_prefetch=1, grid=(n_seq,),
            in_specs=[pl.BlockSpec(memory_space=pl.ANY)],
            out_specs=pl.BlockSpec((1, N_KV_HEADS, n_blk, HEAD_DIM), lambda b, pt: (b, 0, 0, 0)),
            scratch_shapes=[pltpu.VMEM((2, n_pages, PAGE_ROWS, HEAD_DIM), F32),
                            pltpu.SemaphoreType.DMA((2,))]),
        compiler_params=_cparams("arbitrary"), name="kbar_sample",
    )(page_table, cache_pages)


def _gate_sample_kernel(q_ref, kbar_ref, idx_ref, gate_ref):
    n_seq, _, n_blk, _ = kbar_ref.shape
    for b in range(n_seq):
        for kh in range(N_KV_HEADS):
            r0 = b * N_ATT_HEADS + kh * Q_PER_KV
            q4 = q_ref[r0:r0 + Q_PER_KV, :].astype(BF16)
            gate_ref[r0:r0 + Q_PER_KV, :] = lax.dot_general(
                q4, kbar_ref[b, kh].astype(BF16), NT_DIMS, preferred_element_type=F32)
    gate = gate_ref[...]
    lane = lax.broadcasted_iota(jnp.int32, gate.shape, 1)
    idx_ref[...] = jnp.zeros_like(idx_ref)
    for r in range(MOBA_TOPK):
        best = jnp.max(gate, axis=1, keepdims=True)
        first = jnp.min(jnp.where(gate == best, lane, n_blk), axis=1, keepdims=True)
        idx_ref[:, r:r + 1] = first
        gate = jnp.where(lane == first, -jnp.inf, gate)


def _gate_sample(q_rows, kbar):
    n_seq, _, n_blk, _ = kbar.shape
    assert n_blk >= MOBA_TOPK
    rows = n_seq * N_ATT_HEADS
    return pl.pallas_call(
        _gate_sample_kernel, out_shape=jax.ShapeDtypeStruct((rows, LANES), jnp.int32), grid=(1,),
        in_specs=[_resident((rows, HEAD_DIM)), _resident(kbar.shape)],
        out_specs=pl.BlockSpec((rows, LANES), lambda i: (0, 0)),
        scratch_shapes=[pltpu.VMEM((rows, n_blk), F32)],
        compiler_params=_cparams("arbitrary"), name="gate_sample",
    )(q_rows, kbar)


def _attn_sample_kernel(pt_ref, idx_ref, q_ref, knew_ref, vnew_ref, g_ref, ck_ref, cv_ref, o_ref,
                        kbuf_ref, vbuf_ref, sem_ref, *, layer):
    b = pl.program_id(0)
    n_sel = N_ATT_HEADS * MOBA_TOPK
    cur = b % 2

    def copies(seq, buf):
        out = []
        for s in range(n_sel):
            blk = idx_ref[seq * n_sel + s]
            for p in range(PAGES_PER_BLOCK):
                page = pt_ref[seq, blk * PAGES_PER_BLOCK + p]
                slot = s * PAGES_PER_BLOCK + p
                kh = (s // MOBA_TOPK) // Q_PER_KV
                out.append(pltpu.make_async_copy(ck_ref.at[layer, page, :, kh, :], kbuf_ref.at[buf, slot],
                                                 sem_ref.at[buf, 0]))
                out.append(pltpu.make_async_copy(cv_ref.at[layer, page, :, kh, :], vbuf_ref.at[buf, slot],
                                                 sem_ref.at[buf, 1]))
        return out

    @pl.when(b == 0)
    def _():
        for cp in copies(0, 0):
            cp.start()

    @pl.when(b + 1 < pl.num_programs(0))
    def _():
        for cp in copies(b + 1, 1 - cur):
            cp.start()

    for cp in copies(b, cur):
        cp.wait()

    pages_per_head = MOBA_TOPK * PAGES_PER_BLOCK
    heads = range(N_ATT_HEADS)
    qs = [q_ref[0, h:h + 1, :].astype(BF16) for h in heads]
    scores = []
    for h in heads:
        q8 = jnp.broadcast_to(qs[h], (SUBLANES, HEAD_DIM))
        scores.append([lax.dot_general(q8, kbuf_ref[cur, h * pages_per_head + s].astype(BF16), NT_DIMS,
                                       preferred_element_type=F32)[0:1, :] for s in range(pages_per_head)])
    probs, denom, own = [], [], []
    for h in heads:
        kh = h // Q_PER_KV
        k_self = knew_ref[0, kh:kh + 1, :].astype(BF16).astype(F32)
        s_self = jnp.sum(qs[h].astype(F32) * k_self, axis=1, keepdims=True)
        m = s_self
        for sc in scores[h]:
            m = jnp.maximum(m, jnp.max(sc, axis=1, keepdims=True))
        p_self = jnp.exp((s_self - m) * ATT_SCALE)
        ps = [jnp.exp((sc - m) * ATT_SCALE) for sc in scores[h]]
        l = p_self
        for p in ps:
            l = l + jnp.sum(p, axis=1, keepdims=True)
        probs.append([jnp.broadcast_to(p.astype(BF16), (SUBLANES, PAGE_SIZE)) for p in ps])
        denom.append(l)
        own.append(p_self.astype(BF16).astype(F32) * vnew_ref[0, kh:kh + 1, :].astype(BF16).astype(F32))
    for h in heads:
        acc = own[h]
        for s in range(pages_per_head):
            page = vbuf_ref[cur, h * pages_per_head + s].astype(BF16)
            acc = acc + jnp.dot(probs[h][s], page, preferred_element_type=F32)[0:1, :]
        o_ref[0, h:h + 1, :] = ((acc / denom[h]) * _silu(g_ref[0, h:h + 1, :])).astype(o_ref.dtype)


def _attn_sample(q3, knew3, vnew3, g3, cache_k_pages, cache_v_pages, page_table, idx_flat, layer):
    n_seq = q3.shape[0]
    n_slots = N_ATT_HEADS * MOBA_TOPK * PAGES_PER_BLOCK
    per_seq = lambda r: pl.BlockSpec((1, r, HEAD_DIM), lambda b, pt, ix: (b, 0, 0))
    return pl.pallas_call(
        functools.partial(_attn_sample_kernel, layer=layer),
        out_shape=jax.ShapeDtypeStruct((n_seq, N_ATT_HEADS, HEAD_DIM), BF16),
        grid_spec=pltpu.PrefetchScalarGridSpec(
            num_scalar_prefetch=2, grid=(n_seq,),
            in_specs=[per_seq(N_ATT_HEADS), per_seq(N_KV_HEADS), per_seq(N_KV_HEADS), per_seq(N_ATT_HEADS),
                      pl.BlockSpec(memory_space=pl.ANY), pl.BlockSpec(memory_space=pl.ANY)],
            out_specs=per_seq(N_ATT_HEADS),
            scratch_shapes=[pltpu.VMEM((2, n_slots, PAGE_SIZE, HEAD_DIM), F32),
                            pltpu.VMEM((2, n_slots, PAGE_SIZE, HEAD_DIM), F32),
                            pltpu.SemaphoreType.DMA((2, 2))]),
        compiler_params=_cparams("arbitrary"), name="attn_sample",
    )(page_table, idx_flat, q3, knew3, vnew3, g3, cache_k_pages, cache_v_pages)


def _column(row):
    return jnp.broadcast_to(row, (LANES, LANES)).T


def _ssd_sample_kernel(xbc_ref, z_ref, dt_ref, sc_ref, ss_ref, convw_ref, convb_ref, dtb_ref, acoef_ref,
                       dskip_ref, ng_ref, o_ref, nc_ref, ns_ref, y_ref):
    hist = sc_ref[0]
    new = xbc_ref[0]
    conv = convb_ref[...] + convw_ref[CONV_WIDTH - 1:CONV_WIDTH, :] * new
    for w in range(CONV_WIDTH - 1):
        conv = conv + convw_ref[w:w + 1, :] * hist[w:w + 1, :]
    nc_ref[0, :CONV_WIDTH - 2, :] = hist[1:, :]
    nc_ref[0, CONV_WIDTH - 2:, :] = new
    xc = _silu(conv)
    xs = xc[:, :SSD_WIDTH]
    dt = jax.nn.softplus(dt_ref[0] + dtb_ref[...])
    d_a = jnp.exp(dt * acoef_ref[...])
    xdt = xs * dt
    heads_per_chunk = LANES // SSD_HEAD_DIM
    chunks = range(SSD_WIDTH // LANES)
    x_cols = [_column(xdt[:, c * LANES:(c + 1) * LANES]) for c in chunks]
    da_cols = [_column(d_a[:, c * LANES:(c + 1) * LANES]) for c in chunks]
    y_cols = []
    for c in chunks:
        parts = []
        for hh in range(heads_per_chunk):
            h = c * heads_per_chunk + hh
            g = h // HEADS_PER_GROUP
            rows = slice(hh * SSD_HEAD_DIM, (hh + 1) * SSD_HEAD_DIM)
            b_g = xc[:, SSD_WIDTH + g * D_STATE:SSD_WIDTH + (g + 1) * D_STATE]
            c_g = xc[:, SSD_WIDTH + (SSD_GROUPS + g) * D_STATE:SSD_WIDTH + (SSD_GROUPS + g + 1) * D_STATE]
            st = ss_ref[0, h] * da_cols[c][rows, :] + x_cols[c][rows, :] * b_g
            ns_ref[0, h] = st
            parts.append(jnp.sum(st * c_g, axis=1, keepdims=True))
        y_cols.append(jnp.concatenate(parts, axis=0))
    for c in chunks:
        y_ref[:, c * LANES:(c + 1) * LANES] = jnp.broadcast_to(y_cols[c], (LANES, LANES)).T[0:1, :]
    y = (y_ref[...] + dskip_ref[...] * xs) * _silu(z_ref[0])
    o_ref[0] = _rmsnorm(y, ng_ref[...]).astype(o_ref.dtype)


def _ssd_sample(xbc3, z3, dte3, state_conv, state_ssm, layer, conv_w, conv_b, dt_bias, a_log, d_skip, ssd_norm_g):
    n_seq = xbc3.shape[0]
    per_channel = lambda v: jnp.repeat(v.astype(F32), SSD_HEAD_DIM).reshape(1, SSD_WIDTH)
    seq3 = lambda w: pl.BlockSpec((1, 1, w), lambda b: (b, 0, 0))
    state_spec = pl.BlockSpec((None, 1, N_SSD_HEADS, SSD_HEAD_DIM, D_STATE), lambda b: (layer, b, 0, 0, 0))
    in_specs = [seq3(CONV_DIM), seq3(SSD_WIDTH), seq3(SSD_WIDTH),
                pl.BlockSpec((None, 1, CONV_WIDTH - 1, CONV_DIM), lambda b: (layer, b, 0, 0)), state_spec,
                _resident((CONV_WIDTH, CONV_DIM)), _resident((1, CONV_DIM)), _resident((1, SSD_WIDTH)),
                _resident((1, SSD_WIDTH)), _resident((1, SSD_WIDTH)), _resident((1, SSD_WIDTH))]
    operands = [xbc3, z3, dte3, state_conv, state_ssm, conv_w, conv_b.reshape(1, CONV_DIM), per_channel(dt_bias),
                per_channel(-jnp.exp(a_log.astype(F32))), per_channel(d_skip), ssd_norm_g.reshape(1, SSD_WIDTH)]
    return pl.pallas_call(
        _ssd_sample_kernel,
        out_shape=(jax.ShapeDtypeStruct((n_seq, 1, SSD_WIDTH), BF16),
                   jax.ShapeDtypeStruct((n_seq, CONV_WIDTH - 1, CONV_DIM), F32),
                   jax.ShapeDtypeStruct((n_seq, N_SSD_HEADS, SSD_HEAD_DIM, D_STATE), F32)),
        grid=(n_seq,), in_specs=in_specs,
        out_specs=(seq3(SSD_WIDTH), pl.BlockSpec((1, CONV_WIDTH - 1, CONV_DIM), lambda b: (b, 0, 0)),
                   pl.BlockSpec((1, N_SSD_HEADS, SSD_HEAD_DIM, D_STATE), lambda b: (b, 0, 0, 0))),
        scratch_shapes=[pltpu.VMEM((1, SSD_WIDTH), F32)],
        compiler_params=_cparams("parallel"), name="ssd_sample",
    )(*operands)


def _sample_layer(x, layer, cache_k, cache_v, state_conv, state_ssm, page_table,
                  norm_g, w_in, conv_w, conv_b, dt_bias, a_log, d_skip, ssd_norm_g, w_out, final_g, final_norm):
    n_seq = x.shape[0]
    u = _in_proj_sample(x, norm_g, w_in)
    q, k, v, g_att = u[:, :Q_END], u[:, Q_END:K_END], u[:, K_END:V_END], u[:, V_END:G_END]
    z, xbc, dte = u[:, G_END:Z_END], u[:, Z_END:XBC_END], u[:, XBC_END:]
    cache_k_pages = cache_k.reshape(cache_k.shape[0], cache_k.shape[1], PAGE_ROWS, HEAD_DIM)
    kbar = _kbar_sample(cache_k_pages, page_table, layer)
    idx = _gate_sample(q.reshape(n_seq * N_ATT_HEADS, HEAD_DIM), kbar)
    heads = lambda a, r: a.reshape(n_seq, r, HEAD_DIM)
    o_att = _attn_sample(heads(q, N_ATT_HEADS), heads(k, N_KV_HEADS), heads(v, N_KV_HEADS),
                         heads(g_att, N_ATT_HEADS), cache_k, cache_v, page_table,
                         idx[:, :MOBA_TOPK].reshape(-1), layer)
    rows3 = lambda a: a.reshape(n_seq, 1, a.shape[1])
    o_ssd, new_conv, new_ssm = _ssd_sample(rows3(xbc), rows3(z), rows3(dte), state_conv, state_ssm, layer,
                                           conv_w, conv_b, dt_bias, a_log, d_skip, ssd_norm_g)
    h = _out_proj(x, o_att.reshape(n_seq, ATT_WIDTH), o_ssd.reshape(n_seq, SSD_WIDTH), w_out.astype(BF16),
                  final_g, final_norm, n_seq)
    return h, k, v, new_conv, new_ssm


PROMPT_ROW_TILE = 512


def kernel(x_prompt, x_sample, cache_k, cache_v, state_conv, state_ssm, page_table, norm_g, w_in, conv_w,
           conv_b, dt_bias, a_log, d_skip, ssd_norm_g, w_out, final_norm_g):
    depth = w_in.shape[0]
    batch, seq, _ = x_prompt.shape
    n_seq, dec_seq, _ = x_sample.shape
    assert batch == 1 and dec_seq == 1
    hp, hs = x_prompt[0], x_sample[:, 0]
    outs = [[] for _ in range(8)]
    for l in range(depth):
        last = l == depth - 1
        params = (norm_g[l], _in_proj_weights(w_in[l]), conv_w[l], conv_b[l], dt_bias[l], a_log[l], d_skip[l],
                  ssd_norm_g[l], w_out[l], final_norm_g, last)
        hp, kp, vp, cp, sp = _prompt_layer(hp, *params, PROMPT_ROW_TILE)
        hs, ks, vs, cs, ss = _sample_layer(hs, l, cache_k, cache_v, state_conv, state_ssm, page_table, *params)
        kv4 = lambda a: a.reshape(n_seq, dec_seq, N_KV_HEADS, HEAD_DIM)
        for lst, val in zip(outs, (kp[None], vp[None], cp[None], sp[None], kv4(ks), kv4(vs), cs, ss)):
            lst.append(val)
    return (hp[None], hs[:, None]) + tuple(jnp.stack(o) for o in outs)
```

```python
import functools

import jax
import jax.numpy as jnp
from jax import lax
from jax.experimental import pallas as pl
from jax.experimental.pallas import tpu as pltpu

F32 = jnp.float32
BF16 = jnp.bfloat16

D_MODEL = 1024
N_ATT_HEADS = 8
N_KV_HEADS = 2
HEAD_DIM = 128
Q_PER_KV = N_ATT_HEADS // N_KV_HEADS
ATT_WIDTH = N_ATT_HEADS * HEAD_DIM
KV_WIDTH = N_KV_HEADS * HEAD_DIM
MOBA_BLOCK = 256
MOBA_TOPK = 3
PAGE_SIZE = 128
SSD_WIDTH = 1024
SSD_HEAD_DIM = 64
N_SSD_HEADS = SSD_WIDTH // SSD_HEAD_DIM
SSD_GROUPS = 2
HEADS_PER_GROUP = N_SSD_HEADS // SSD_GROUPS
GROUP_WIDTH = HEADS_PER_GROUP * SSD_HEAD_DIM
D_STATE = 128
CONV_WIDTH = 4
SSD_CHUNK = 256
CONV_DIM = SSD_WIDTH + 2 * SSD_GROUPS * D_STATE
MIX_WIDTH = ATT_WIDTH + SSD_WIDTH
Q_END = ATT_WIDTH
K_END = Q_END + KV_WIDTH
V_END = K_END + KV_WIDTH
G_END = V_END + ATT_WIDTH
Z_END = G_END + SSD_WIDTH
XBC_END = Z_END + CONV_DIM
IN_WIDTH = XBC_END + N_SSD_HEADS
NORM_EPS = 1e-5
ATT_SCALE = HEAD_DIM ** -0.5
EXP2_SCALE = ATT_SCALE * 1.4426950408889634

LANES = 128
SUBLANES = 8
VMEM_LIMIT_BYTES = 56 * 1024 * 1024

MASK_BIAS = -1e30
VT_ONES = 16
VT_ROWS = HEAD_DIM + VT_ONES
LOGITS_LEAD = 1
KEY_BLOCKS_PER_STEP = 4

NT_DIMS = (((1,), (1,)), ((), ()))
TN_DIMS = (((0,), (0,)), ((), ()))


def _cparams(*semantics):
    return pltpu.CompilerParams(dimension_semantics=semantics, vmem_limit_bytes=VMEM_LIMIT_BYTES)


def _resident(shape):
    nd = len(shape)
    return pl.BlockSpec(shape, lambda *_: (0,) * nd, pipeline_mode=pl.Buffered(1))


def _rmsnorm(x, g):
    return x * lax.rsqrt(jnp.mean(x * x, axis=-1, keepdims=True) + NORM_EPS) * g


def _silu(x):
    return x * jax.nn.sigmoid(x)


def _reduce_rows(x, op, final):
    slabs = [x[r:r + SUBLANES] for r in range(0, x.shape[0], SUBLANES)]
    while len(slabs) > 1:
        nxt = [op(slabs[a], slabs[a + 1]) for a in range(0, len(slabs) - 1, 2)]
        if len(slabs) % 2:
            nxt.append(slabs[-1])
        slabs = nxt
    return final(slabs[0], axis=0, keepdims=True)


def _split3(x):
    hi = x.astype(BF16)
    r = x - hi.astype(F32)
    mid = r.astype(BF16)
    lo = (r - mid.astype(F32)).astype(BF16)
    return hi, mid, lo


def _dot3(parts, m, left=False):
    if left:
        f = lambda p: jnp.dot(m, p, preferred_element_type=F32)
    else:
        f = lambda p: jnp.dot(p, m, preferred_element_type=F32)
    return (f(parts[2]) + f(parts[1])) + f(parts[0])


def _in_proj_kernel(x_ref, g_ref, wqT_ref, wkv_ref, wvT_ref, wrest_ref, wdt_ref,
                    qT_ref, k_ref, v_ref, ka_ref, vT_ref, gz_ref, xbc_ref, dt_ref, ksum_ref):
    tm = x_ref.shape[0]
    xn = _rmsnorm(x_ref[...], g_ref[...]).astype(BF16)
    qT = lax.dot_general(wqT_ref[...], xn, NT_DIMS, preferred_element_type=F32)
    qT_ref[...] = (qT * EXP2_SCALE).astype(BF16)
    vT = lax.dot_general(wvT_ref[...], xn, NT_DIMS, preferred_element_type=F32).astype(BF16)
    kv = jnp.dot(xn, wkv_ref[...], preferred_element_type=F32)
    k = kv[:, :KV_WIDTH]
    for h in range(N_KV_HEADS):
        k_ref[:, h, :] = kv[:, h * HEAD_DIM:(h + 1) * HEAD_DIM]
        v_ref[:, h, :] = kv[:, KV_WIDTH + h * HEAD_DIM:KV_WIDTH + (h + 1) * HEAD_DIM]
    blk0 = pl.program_id(0) * (tm // MOBA_BLOCK)
    lane = lax.broadcasted_iota(jnp.int32, (MOBA_BLOCK, LANES), 1)
    for b in range(tm // MOBA_BLOCK):
        rows = slice(b * MOBA_BLOCK, (b + 1) * MOBA_BLOCK)
        onehot = (lane == blk0 + b).astype(BF16)
        for h in range(N_KV_HEADS):
            ka_ref[h, b, :, :HEAD_DIM] = k[rows, h * HEAD_DIM:(h + 1) * HEAD_DIM].astype(BF16)
            ka_ref[h, b, :, HEAD_DIM:] = onehot
        for h in range(N_KV_HEADS):
            vT_ref[b, h * VT_ROWS:h * VT_ROWS + HEAD_DIM, :] = vT[h * HEAD_DIM:(h + 1) * HEAD_DIM, rows]
            vT_ref[b, h * VT_ROWS + HEAD_DIM:(h + 1) * VT_ROWS, :] = jnp.ones((VT_ONES, MOBA_BLOCK), BF16)
        ksum_ref[0, b:b + 1, :] = jnp.sum(k[rows], axis=0, keepdims=True)
    rest = jnp.dot(xn, wrest_ref[...], preferred_element_type=F32)
    gz_ref[...] = rest[:, :ATT_WIDTH + SSD_WIDTH]
    xbc_ref[...] = rest[:, ATT_WIDTH + SSD_WIDTH:]
    dt_ref[...] = jnp.dot(xn, wdt_ref[...], preferred_element_type=F32)


def _in_proj_weights(w_in):
    wb = w_in.astype(BF16)
    wqT = wb[:, :Q_END].T
    wkv = wb[:, Q_END:V_END]
    wvT = wb[:, K_END:V_END].T
    wrest = wb[:, V_END:XBC_END]
    wdt = jnp.pad(wb[:, XBC_END:], ((0, 0), (0, LANES - N_SSD_HEADS)))
    return wqT, wkv, wvT, wrest, wdt


def _in_proj(x, norm_g, weights, tm):
    n = x.shape[0]
    assert n % tm == 0 and tm % MOBA_BLOCK == 0
    wqT, wkv, wvT, wrest, wdt = weights
    nb = tm // MOBA_BLOCK
    row = lambda i: (i, 0)
    kv_spec = pl.BlockSpec((tm, N_KV_HEADS, HEAD_DIM), lambda i: (i, 0, 0))
    out_shape = (
        jax.ShapeDtypeStruct((ATT_WIDTH, n), BF16),
        jax.ShapeDtypeStruct((n, N_KV_HEADS, HEAD_DIM), F32),
        jax.ShapeDtypeStruct((n, N_KV_HEADS, HEAD_DIM), F32),
        jax.ShapeDtypeStruct((N_KV_HEADS, n // MOBA_BLOCK, MOBA_BLOCK, 2 * HEAD_DIM), BF16),
        jax.ShapeDtypeStruct((n // MOBA_BLOCK, N_KV_HEADS * VT_ROWS, MOBA_BLOCK), BF16),
        jax.ShapeDtypeStruct((n, ATT_WIDTH + SSD_WIDTH), F32),
        jax.ShapeDtypeStruct((n, CONV_DIM), F32),
        jax.ShapeDtypeStruct((n, LANES), F32),
        jax.ShapeDtypeStruct((n // tm, nb, KV_WIDTH), F32),
    )
    out_specs = (
        pl.BlockSpec((ATT_WIDTH, tm), lambda i: (0, i)),
        kv_spec,
        kv_spec,
        pl.BlockSpec((N_KV_HEADS, nb, MOBA_BLOCK, 2 * HEAD_DIM), lambda i: (0, i, 0, 0)),
        pl.BlockSpec((nb, N_KV_HEADS * VT_ROWS, MOBA_BLOCK), lambda i: (i, 0, 0)),
        pl.BlockSpec((tm, ATT_WIDTH + SSD_WIDTH), row),
        pl.BlockSpec((tm, CONV_DIM), row),
        pl.BlockSpec((tm, LANES), row),
        pl.BlockSpec((1, nb, KV_WIDTH), lambda i: (i, 0, 0)),
    )
    in_specs = [pl.BlockSpec((tm, D_MODEL), row), _resident((1, D_MODEL)), _resident(wqT.shape),
                _resident(wkv.shape), _resident(wvT.shape), _resident(wrest.shape), _resident(wdt.shape)]
    return pl.pallas_call(
        _in_proj_kernel, out_shape=out_shape, grid=(n // tm,), in_specs=in_specs, out_specs=out_specs,
        compiler_params=_cparams("parallel"), name="in_proj",
    )(x, norm_g.reshape(1, D_MODEL), wqT, wkv, wvT, wrest, wdt)


def _moba_prompt_kernel(qT_ref, ka_ref, vT_ref, ksum_ref, g_ref, o_ref, qa_ref, m_ref, acc_ref,
                        s0_ref, c0_ref, s1_ref, c1_ref):
    i = pl.program_id(1)
    n_blk = ksum_ref.shape[0]
    kbar = (ksum_ref[...] * (1.0 / MOBA_BLOCK)).astype(BF16)
    blk = lax.broadcasted_iota(jnp.int32, (n_blk, MOBA_BLOCK), 0)
    key_pos = lax.broadcasted_iota(jnp.int32, (MOBA_BLOCK, MOBA_BLOCK), 0)
    qry_pos = lax.broadcasted_iota(jnp.int32, (MOBA_BLOCK, MOBA_BLOCK), 1)

    for h in range(Q_PER_KV):
        cols = slice(h * MOBA_BLOCK, (h + 1) * MOBA_BLOCK)
        qT = qT_ref[h * HEAD_DIM:(h + 1) * HEAD_DIM, :]
        gate = jnp.dot(kbar, qT, preferred_element_type=F32)
        gate = jnp.where(blk < i, gate, -jnp.inf)
        bias = jnp.full((n_blk, MOBA_BLOCK), MASK_BIAS, F32)
        for r in range(MOBA_TOPK):
            best = jnp.max(gate, axis=0, keepdims=True)
            first = jnp.min(jnp.where(gate == best, blk, n_blk), axis=0, keepdims=True)
            first = jnp.where(r < i, first, -1)
            pick = blk == first
            bias = jnp.where(pick, 0.0, bias)
            gate = jnp.where(pick, -jnp.inf, gate)
        qa_ref[:HEAD_DIM, cols] = qT
        qa_ref[HEAD_DIM:HEAD_DIM + n_blk, cols] = bias.astype(BF16)
        if n_blk < LANES:
            qa_ref[HEAD_DIM + n_blk:, cols] = jnp.zeros((LANES - n_blk, MOBA_BLOCK), BF16)

    def own_block():
        k_own = ka_ref[i][:, :HEAD_DIM]
        logit = [jnp.dot(k_own, qT_ref[h * HEAD_DIM:(h + 1) * HEAD_DIM, :], preferred_element_type=F32)
                 for h in range(Q_PER_KV)]
        probs = []
        for h in range(Q_PER_KV):
            s = jnp.where(key_pos <= qry_pos, logit[h], MASK_BIAS)
            m_new = _reduce_rows(s, jnp.maximum, jnp.max)
            m_ref[h] = m_new
            probs.append(jnp.exp2(s - m_new).astype(BF16))
        for h in range(Q_PER_KV):
            acc_ref[h] = jnp.dot(vT_ref[i], probs[h], preferred_element_type=F32)

    kb = KEY_BLOCKS_PER_STEP
    n_groups = (i + kb - 1) // kb
    last_group = n_groups - 1

    def keys_of(group):
        return ka_ref[pl.ds(kb * group, kb)].reshape(kb * MOBA_BLOCK, 2 * HEAD_DIM)

    def head_logits(k_aug, h, dst_ref, dmax_ref):
        s = jnp.dot(k_aug, qa_ref[:, h * MOBA_BLOCK:(h + 1) * MOBA_BLOCK], preferred_element_type=F32)
        dst_ref[h] = s
        dmax_ref[h] = _reduce_rows(s, jnp.maximum, jnp.max)

    def logits(group, dst_ref, dmax_ref):
        k_aug = keys_of(group)
        for h in range(Q_PER_KV):
            head_logits(k_aug, h, dst_ref, dmax_ref)

    def softmax_step(group, cur_ref, cmax_ref, nxt_ref, nmax_ref):
        k_next = keys_of(jnp.minimum(group + 1, last_group))
        vT = jnp.concatenate([vT_ref[kb * group + b] for b in range(kb)], axis=1)
        for h in range(LOGITS_LEAD):
            head_logits(k_next, h, nxt_ref, nmax_ref)
        for h in range(Q_PER_KV):
            if h + LOGITS_LEAD < Q_PER_KV:
                head_logits(k_next, h + LOGITS_LEAD, nxt_ref, nmax_ref)
            m_old = m_ref[h]
            m_new = jnp.maximum(m_old, cmax_ref[h])
            p = jnp.exp2(cur_ref[h] - m_new).astype(BF16)
            acc_ref[h] = jnp.exp2(m_old - m_new) * acc_ref[h] + jnp.dot(vT, p, preferred_element_type=F32)
            m_ref[h] = m_new

    own_block()
    logits(0, s0_ref, c0_ref)

    def one_group(group, carry):
        @pl.when(group % 2 == 0)
        def _():
            softmax_step(group, s0_ref, c0_ref, s1_ref, c1_ref)

        @pl.when(group % 2 == 1)
        def _():
            softmax_step(group, s1_ref, c1_ref, s0_ref, c0_ref)

        return carry

    lax.fori_loop(0, n_groups, one_group, 0)

    for h in range(Q_PER_KV):
        acc = acc_ref[h]
        out = (acc[:HEAD_DIM] / acc[HEAD_DIM:HEAD_DIM + 1]).T
        dcols = slice(h * HEAD_DIM, (h + 1) * HEAD_DIM)
        o_ref[:, dcols] = (out * _silu(g_ref[:, dcols])).astype(BF16)


def _moba_prompt(qT, ka, vT, ksum, gz):
    n = qT.shape[1]
    n_blk = n // MOBA_BLOCK
    assert n_blk <= LANES and n_blk % KEY_BLOCKS_PER_STEP == 0
    kv_rows = Q_PER_KV * HEAD_DIM
    logit_buf = pltpu.VMEM((Q_PER_KV, KEY_BLOCKS_PER_STEP * MOBA_BLOCK, MOBA_BLOCK), F32)
    stat_buf = pltpu.VMEM((Q_PER_KV, 1, MOBA_BLOCK), F32)
    return pl.pallas_call(
        _moba_prompt_kernel,
        out_shape=jax.ShapeDtypeStruct((n, ATT_WIDTH), BF16),
        grid=(N_KV_HEADS, n_blk),
        in_specs=[
            pl.BlockSpec((kv_rows, MOBA_BLOCK), lambda h, i: (h, i)),
            pl.BlockSpec((None, n_blk, MOBA_BLOCK, 2 * HEAD_DIM), lambda h, i: (h, 0, 0, 0)),
            pl.BlockSpec((n_blk, VT_ROWS, MOBA_BLOCK), lambda h, i: (0, h, 0)),
            pl.BlockSpec((n_blk, HEAD_DIM), lambda h, i: (0, h)),
            pl.BlockSpec((MOBA_BLOCK, kv_rows), lambda h, i: (i, h)),
        ],
        out_specs=pl.BlockSpec((MOBA_BLOCK, kv_rows), lambda h, i: (i, h)),
        scratch_shapes=[pltpu.VMEM((2 * HEAD_DIM, Q_PER_KV * MOBA_BLOCK), BF16),
                        stat_buf,
                        pltpu.VMEM((Q_PER_KV, VT_ROWS, MOBA_BLOCK), F32),
                        logit_buf, stat_buf, logit_buf, stat_buf],
        compiler_params=_cparams("parallel", "parallel"), name="moba_prompt",
    )(qT, ka, vT, ksum, gz)


def _ssd_prompt_kernel(xbc_ref, z_ref, dt_ref, convw_ref, convb_ref, dtb_ref, acoef_ref, dskipe_ref,
                       ng_ref, o_ref, state_ref, xpad_ref, st_ref, y_ref):
    c = pl.program_id(0)
    t = SSD_CHUNK

    @pl.when(c == 0)
    def _():
        xpad_ref[:SUBLANES, :] = jnp.zeros((SUBLANES, CONV_DIM), F32)
        st_ref[...] = jnp.zeros_like(st_ref)

    xpad_ref[SUBLANES:, :] = xbc_ref[...]
    conv = convb_ref[...] + convw_ref[CONV_WIDTH - 1:CONV_WIDTH, :] * xpad_ref[SUBLANES:, :]
    for w in range(CONV_WIDTH - 1):
        back = CONV_WIDTH - 1 - w
        conv = conv + convw_ref[w:w + 1, :] * xpad_ref[SUBLANES - back:SUBLANES - back + t, :]
    xpad_ref[:SUBLANES, :] = xpad_ref[t:t + SUBLANES, :]
    xc = _silu(conv)
    xs = xc[:, :SSD_WIDTH]

    dt = jax.nn.softplus(dt_ref[...] + dtb_ref[...])
    head_of_channel = lax.broadcasted_iota(jnp.int32, (LANES, SSD_WIDTH), 1) // SSD_HEAD_DIM
    expand = (head_of_channel == lax.broadcasted_iota(jnp.int32, (LANES, SSD_WIDTH), 0)).astype(BF16)
    dt_e = _dot3(_split3(dt), expand)
    lower = lax.broadcasted_iota(jnp.int32, (t, t), 0) >= lax.broadcasted_iota(jnp.int32, (t, t), 1)
    tril = lower.astype(BF16)
    cumsum = lambda v: _dot3(_split3(v), tril, left=True)
    a_cum = cumsum(dt * acoef_ref[...])
    a_cum_t = a_cum.T
    a_cum_e = _dot3(_split3(a_cum), expand)
    a_end_e = a_cum_e[t - 1:t, :]

    xdt = xs * dt_e
    xdt_end = (xdt * jnp.exp(a_end_e - a_cum_e)).astype(BF16)
    from_start = jnp.exp(a_cum_e)
    state_decay = jnp.exp(a_end_e)
    lane = lax.broadcasted_iota(jnp.int32, (t, LANES), 1)

    for g in range(SSD_GROUPS):
        gc = slice(g * GROUP_WIDTH, (g + 1) * GROUP_WIDTH)
        b_bf = xc[:, SSD_WIDTH + g * D_STATE:SSD_WIDTH + (g + 1) * D_STATE].astype(BF16)
        c_bf = xc[:, SSD_WIDTH + (SSD_GROUPS + g) * D_STATE:
                  SSD_WIDTH + (SSD_GROUPS + g + 1) * D_STATE].astype(BF16)
        cb = lax.dot_general(c_bf, b_bf, NT_DIMS, preferred_element_type=F32)
        st = st_ref[g]
        y_ref[:, gc] = jnp.dot(c_bf, st.astype(BF16), preferred_element_type=F32) * from_start[:, gc]
        st_ref[g] = st * state_decay[:, gc] + lax.dot_general(b_bf, xdt_end[:, gc], TN_DIMS,
                                                               preferred_element_type=F32)
        for pair in range(GROUP_WIDTH // LANES):
            pc = slice(g * GROUP_WIDTH + pair * LANES, g * GROUP_WIDTH + (pair + 1) * LANES)
            xdt_pair = xdt[:, pc]
            y_pair = y_ref[:, pc] + dskipe_ref[:, pc] * xs[:, pc]
            for hh in range(LANES // SSD_HEAD_DIM):
                h = (g * GROUP_WIDTH + pair * LANES) // SSD_HEAD_DIM + hh
                decay = jnp.exp(jnp.where(lower, a_cum[:, h:h + 1] - a_cum_t[h:h + 1, :], -jnp.inf))
                mine = (lane // SSD_HEAD_DIM) == hh
                rhs = jnp.where(mine, xdt_pair, 0.0).astype(BF16)
                y_pair = y_pair + jnp.dot((cb * decay).astype(BF16), rhs, preferred_element_type=F32)
            y_ref[:, pc] = y_pair

    y = y_ref[...] * _silu(z_ref[...])
    o_ref[...] = _rmsnorm(y, ng_ref[...]).astype(o_ref.dtype)

    @pl.when(c == pl.num_programs(0) - 1)
    def _():
        for g in range(SSD_GROUPS):
            for pair in range(GROUP_WIDTH // LANES):
                both = st_ref[g][:, pair * LANES:(pair + 1) * LANES].T
                for hh in range(LANES // SSD_HEAD_DIM):
                    h = (g * GROUP_WIDTH + pair * LANES) // SSD_HEAD_DIM + hh
                    state_ref[h] = both[hh * SSD_HEAD_DIM:(hh + 1) * SSD_HEAD_DIM, :]


def _ssd_prompt(xbc, gz, dt_raw, conv_w, conv_b, dt_bias, a_log, d_skip, ssd_norm_g):
    n = xbc.shape[0]
    t = SSD_CHUNK
    assert n % t == 0
    pad = lambda v: jnp.pad(v.reshape(1, N_SSD_HEADS), ((0, 0), (0, LANES - N_SSD_HEADS)))
    per_channel = lambda v: jnp.repeat(v.astype(F32), SSD_HEAD_DIM).reshape(1, SSD_WIDTH)
    a_coef = -jnp.exp(a_log.astype(F32))
    row = lambda c: (c, 0)
    return pl.pallas_call(
        _ssd_prompt_kernel,
        out_shape=(jax.ShapeDtypeStruct((n, SSD_WIDTH), BF16),
                   jax.ShapeDtypeStruct((N_SSD_HEADS, SSD_HEAD_DIM, D_STATE), F32)),
        grid=(n // t,),
        in_specs=[pl.BlockSpec((t, CONV_DIM), row),
                  pl.BlockSpec((t, SSD_WIDTH), lambda c: (c, 1)),
                  pl.BlockSpec((t, LANES), row),
                  _resident((CONV_WIDTH, CONV_DIM)), _resident((1, CONV_DIM)), _resident((1, LANES)),
                  _resident((1, LANES)), _resident((1, SSD_WIDTH)), _resident((1, SSD_WIDTH))],
        out_specs=(pl.BlockSpec((t, SSD_WIDTH), row),
                   pl.BlockSpec((N_SSD_HEADS, SSD_HEAD_DIM, D_STATE), lambda c: (0, 0, 0))),
        scratch_shapes=[pltpu.VMEM((t + SUBLANES, CONV_DIM), F32),
                        pltpu.VMEM((SSD_GROUPS, D_STATE, GROUP_WIDTH), F32),
                        pltpu.VMEM((t, SSD_WIDTH), F32)],
        compiler_params=_cparams("arbitrary"), name="ssd_prompt",
    )(xbc, gz, dt_raw, conv_w, conv_b.reshape(1, CONV_DIM), pad(dt_bias), pad(a_coef), per_channel(d_skip),
      ssd_norm_g.reshape(1, SSD_WIDTH))


def _out_proj_kernel(x_ref, oa_ref, os_ref, w_ref, fg_ref, h_ref, *, final_norm):
    out = jnp.dot(oa_ref[...], w_ref[:ATT_WIDTH, :], preferred_element_type=F32)
    out = out + jnp.dot(os_ref[...], w_ref[ATT_WIDTH:, :], preferred_element_type=F32)
    h = x_ref[...] + out
    h_ref[...] = _rmsnorm(h, fg_ref[...]) if final_norm else h


def _out_proj(x, o_att, o_ssd, w_out_bf, final_g, final_norm, tm):
    n = x.shape[0]
    assert n % tm == 0
    row = lambda i: (i, 0)
    return pl.pallas_call(
        functools.partial(_out_proj_kernel, final_norm=final_norm),
        out_shape=jax.ShapeDtypeStruct((n, D_MODEL), F32),
        grid=(n // tm,),
        in_specs=[pl.BlockSpec((tm, D_MODEL), row), pl.BlockSpec((tm, ATT_WIDTH), row),
                  pl.BlockSpec((tm, SSD_WIDTH), row), _resident((MIX_WIDTH, D_MODEL)), _resident((1, D_MODEL))],
        out_specs=pl.BlockSpec((tm, D_MODEL), row),
        compiler_params=_cparams("parallel"), name="out_proj",
    )(x, o_att, o_ssd, w_out_bf, final_g.reshape(1, D_MODEL))


def _prompt_layer(x, norm_g, w_in, conv_w, conv_b, dt_bias, a_log, d_skip, ssd_norm_g, w_out, final_g,
                  final_norm, tm):
    n = x.shape[0]
    qT, k, v, ka, vT, gz, xbc, dt_raw, ksum = _in_proj(x, norm_g, w_in, tm)
    o_att = _moba_prompt(qT, ka, vT, ksum.reshape(n // MOBA_BLOCK, KV_WIDTH), gz)
    o_ssd, state = _ssd_prompt(xbc, gz, dt_raw, conv_w, conv_b, dt_bias, a_log, d_skip, ssd_norm_g)
    h = _out_proj(x, o_att, o_ssd, w_out.astype(BF16), final_g, final_norm, tm)
    return h, k, v, xbc[n - (CONV_WIDTH - 1):], state


PAGES_PER_BLOCK = MOBA_BLOCK // PAGE_SIZE
PAGE_ROWS = PAGE_SIZE * N_KV_HEADS
SAMPLE_U_WIDTH = XBC_END + SSD_WIDTH


def _in_proj_sample_kernel(x_ref, g_ref, wqT_ref, wkv_ref, wrest_ref, wdte_ref, u_ref):
    xn = _rmsnorm(x_ref[...], g_ref[...]).astype(BF16)
    u_ref[:, :Q_END] = lax.dot_general(xn, wqT_ref[...], NT_DIMS, preferred_element_type=F32)
    u_ref[:, Q_END:V_END] = jnp.dot(xn, wkv_ref[...], preferred_element_type=F32)
    u_ref[:, V_END:XBC_END] = jnp.dot(xn, wrest_ref[...], preferred_element_type=F32)
    u_ref[:, XBC_END:] = jnp.dot(xn, wdte_ref[...], preferred_element_type=F32)


def _in_proj_sample(x, norm_g, weights):
    n = x.shape[0]
    wqT, wkv, _, wrest, wdt = weights
    wdte = jnp.repeat(wdt[:, :N_SSD_HEADS], SSD_HEAD_DIM, axis=1)
    return pl.pallas_call(
        _in_proj_sample_kernel, out_shape=jax.ShapeDtypeStruct((n, SAMPLE_U_WIDTH), F32), grid=(1,),
        in_specs=[_resident((n, D_MODEL)), _resident((1, D_MODEL)), _resident(wqT.shape), _resident(wkv.shape),
                  _resident(wrest.shape), _resident(wdte.shape)],
        out_specs=pl.BlockSpec((n, SAMPLE_U_WIDTH), lambda i: (0, 0)),
        compiler_params=_cparams("arbitrary"), name="in_proj_sample",
    )(x, norm_g.reshape(1, D_MODEL), wqT, wkv, wrest, wdte)


def _kbar_sample_kernel(pt_ref, cache_ref, out_ref, buf_ref, sem_ref, *, layer):
    b = pl.program_id(0)
    n_pages = pt_ref.shape[1]
    slot = b % 2

    def copies(seq, slot):
        return [pltpu.make_async_copy(cache_ref.at[layer, pt_ref[seq, p]], buf_ref.at[slot, p], sem_ref.at[slot])
                for p in range(n_pages)]

    @pl.when(b == 0)
    def _():
        for cp in copies(0, 0):
            cp.start()

    @pl.when(b + 1 < pl.num_programs(0))
    def _():
        for cp in copies(b + 1, 1 - slot):
            cp.start()

    for cp in copies(b, slot):
        cp.wait()
    for blk in range(n_pages // PAGES_PER_BLOCK):
        acc = jnp.zeros((SUBLANES, HEAD_DIM), F32)
        for p in range(PAGES_PER_BLOCK):
            page = buf_ref[slot, blk * PAGES_PER_BLOCK + p]
            acc = acc + jnp.sum(page.reshape(PAGE_ROWS // SUBLANES, SUBLANES, HEAD_DIM), axis=0)
        acc = acc + pltpu.roll(acc, 2, 0)
        acc = acc + pltpu.roll(acc, 4, 0)
        for h in range(N_KV_HEADS):
            out_ref[0, h, blk:blk + 1, :] = acc[h:h + 1, :] * (1.0 / MOBA_BLOCK)


def _kbar_sample(cache_pages, page_table, layer):
    n_seq, n_pages = page_table.shape
    assert n_pages % PAGES_PER_BLOCK == 0
    n_blk = n_pages // PAGES_PER_BLOCK
    return pl.pallas_call(
        functools.partial(_kbar_sample_kernel, layer=layer),
        out_shape=jax.ShapeDtypeStruct((n_seq, N_KV_HEADS, n_blk, HEAD_DIM), F32),
        grid_spec=pltpu.PrefetchScalarGridSpec(
            num_scalar_prefetch=1, grid=(n_seq,),
            in_specs=[pl.BlockSpec(memory_space=pl.ANY)],
            out_specs=pl.BlockSpec((1, N_KV_HEADS, n_blk, HEAD_DIM), lambda b, pt: (b, 0, 0, 0)),
            scratch_shapes=[pltpu.VMEM((2, n_pages, PAGE_ROWS, HEAD_DIM), F32),
                            pltpu.SemaphoreType.DMA((2,))]),
        compiler_params=_cparams("arbitrary"), name="kbar_sample",
    )(page_table, cache_pages)


def _gate_sample_kernel(q_ref, kbar_ref, idx_ref, gate_ref):
    n_seq, _, n_blk, _ = kbar_ref.shape
    for b in range(n_seq):
        for kh in range(N_KV_HEADS):
            r0 = b * N_ATT_HEADS + kh * Q_PER_KV
            q4 = q_ref[r0:r0 + Q_PER_KV, :].astype(BF16)
            gate_ref[r0:r0 + Q_PER_KV, :] = lax.dot_general(
                q4, kbar_ref[b, kh].astype(BF16), NT_DIMS, preferred_element_type=F32)
    gate = gate_ref[...]
    lane = lax.broadcasted_iota(jnp.int32, gate.shape, 1)
    idx_ref[...] = jnp.zeros_like(idx_ref)
    for r in range(MOBA_TOPK):
        best = jnp.max(gate, axis=1, keepdims=True)
        first = jnp.min(jnp.where(gate == best, lane, n_blk), axis=1, keepdims=True)
        idx_ref[:, r:r + 1] = first
        gate = jnp.where(lane == first, -jnp.inf, gate)


def _gate_sample(q_rows, kbar):
    n_seq, _, n_blk, _ = kbar.shape
    assert n_blk >= MOBA_TOPK
    rows = n_seq * N_ATT_HEADS
    return pl.pallas_call(
        _gate_sample_kernel, out_shape=jax.ShapeDtypeStruct((rows, LANES), jnp.int32), grid=(1,),
        in_specs=[_resident((rows, HEAD_DIM)), _resident(kbar.shape)],
        out_specs=pl.BlockSpec((rows, LANES), lambda i: (0, 0)),
        scratch_shapes=[pltpu.VMEM((rows, n_blk), F32)],
        compiler_params=_cparams("arbitrary"), name="gate_sample",
    )(q_rows, kbar)


def _attn_sample_kernel(pt_ref, idx_ref, q_ref, knew_ref, vnew_ref, g_ref, ck_ref, cv_ref, o_ref,
                        kbuf_ref, vbuf_ref, sem_ref, *, layer):
    b = pl.program_id(0)
    n_sel = N_ATT_HEADS * MOBA_TOPK
    cur = b % 2

    def copies(seq, buf):
        out = []
        for s in range(n_sel):
            blk = idx_ref[seq * n_sel + s]
            for p in range(PAGES_PER_BLOCK):
                page = pt_ref[seq, blk * PAGES_PER_BLOCK + p]
                slot = s * PAGES_PER_BLOCK + p
                kh = (s // MOBA_TOPK) // Q_PER_KV
                out.append(pltpu.make_async_copy(ck_ref.at[layer, page, :, kh, :], kbuf_ref.at[buf, slot],
                                                 sem_ref.at[buf, 0]))
                out.append(pltpu.make_async_copy(cv_ref.at[layer, page, :, kh, :], vbuf_ref.at[buf, slot],
                                                 sem_ref.at[buf, 1]))
        return out

    @pl.when(b == 0)
    def _():
        for cp in copies(0, 0):
            cp.start()

    @pl.when(b + 1 < pl.num_programs(0))
    def _():
        for cp in copies(b + 1, 1 - cur):
            cp.start()

    for cp in copies(b, cur):
        cp.wait()

    pages_per_head = MOBA_TOPK * PAGES_PER_BLOCK
    heads = range(N_ATT_HEADS)
    qs = [q_ref[0, h:h + 1, :].astype(BF16) for h in heads]
    scores = []
    for h in heads:
        q8 = jnp.broadcast_to(qs[h], (SUBLANES, HEAD_DIM))
        scores.append([lax.dot_general(q8, kbuf_ref[cur, h * pages_per_head + s].astype(BF16), NT_DIMS,
                                       preferred_element_type=F32)[0:1, :] for s in range(pages_per_head)])
    probs, denom, own = [], [], []
    for h in heads:
        kh = h // Q_PER_KV
        k_self = knew_ref[0, kh:kh + 1, :].astype(BF16).astype(F32)
        s_self = jnp.sum(qs[h].astype(F32) * k_self, axis=1, keepdims=True)
        m = s_self
        for sc in scores[h]:
            m = jnp.maximum(m, jnp.max(sc, axis=1, keepdims=True))
        p_self = jnp.exp((s_self - m) * ATT_SCALE)
        ps = [jnp.exp((sc - m) * ATT_SCALE) for sc in scores[h]]
        l = p_self
        for p in ps:
            l = l + jnp.sum(p, axis=1, keepdims=True)
        probs.append([jnp.broadcast_to(p.astype(BF16), (SUBLANES, PAGE_SIZE)) for p in ps])
        denom.append(l)
        own.append(p_self.astype(BF16).astype(F32) * vnew_ref[0, kh:kh + 1, :].astype(BF16).astype(F32))
    for h in heads:
        acc = own[h]
        for s in range(pages_per_head):
            page = vbuf_ref[cur, h * pages_per_head + s].astype(BF16)
            acc = acc + jnp.dot(probs[h][s], page, preferred_element_type=F32)[0:1, :]
        o_ref[0, h:h + 1, :] = ((acc / denom[h]) * _silu(g_ref[0, h:h + 1, :])).astype(o_ref.dtype)


def _attn_sample(q3, knew3, vnew3, g3, cache_k_pages, cache_v_pages, page_table, idx_flat, layer):
    n_seq = q3.shape[0]
    n_slots = N_ATT_HEADS * MOBA_TOPK * PAGES_PER_BLOCK
    per_seq = lambda r: pl.BlockSpec((1, r, HEAD_DIM), lambda b, pt, ix: (b, 0, 0))
    return pl.pallas_call(
        functools.partial(_attn_sample_kernel, layer=layer),
        out_shape=jax.ShapeDtypeStruct((n_seq, N_ATT_HEADS, HEAD_DIM), BF16),
        grid_spec=pltpu.PrefetchScalarGridSpec(
            num_scalar_prefetch=2, grid=(n_seq,),
            in_specs=[per_seq(N_ATT_HEADS), per_seq(N_KV_HEADS), per_seq(N_KV_HEADS), per_seq(N_ATT_HEADS),
                      pl.BlockSpec(memory_space=pl.ANY), pl.BlockSpec(memory_space=pl.ANY)],
            out_specs=per_seq(N_ATT_HEADS),
            scratch_shapes=[pltpu.VMEM((2, n_slots, PAGE_SIZE, HEAD_DIM), F32),
                            pltpu.VMEM((2, n_slots, PAGE_SIZE, HEAD_DIM), F32),
                            pltpu.SemaphoreType.DMA((2, 2))]),
        compiler_params=_cparams("arbitrary"), name="attn_sample",
    )(page_table, idx_flat, q3, knew3, vnew3, g3, cache_k_pages, cache_v_pages)


def _column(row):
    return jnp.broadcast_to(row, (LANES, LANES)).T


def _ssd_sample_kernel(xbc_ref, z_ref, dt_ref, sc_ref, ss_ref, convw_ref, convb_ref, dtb_ref, acoef_ref,
                       dskip_ref, ng_ref, o_ref, nc_ref, ns_ref, y_ref):
    hist = sc_ref[0]
    new = xbc_ref[0]
    conv = convb_ref[...] + convw_ref[CONV_WIDTH - 1:CONV_WIDTH, :] * new
    for w in range(CONV_WIDTH - 1):
        conv = conv + convw_ref[w:w + 1, :] * hist[w:w + 1, :]
    nc_ref[0, :CONV_WIDTH - 2, :] = hist[1:, :]
    nc_ref[0, CONV_WIDTH - 2:, :] = new
    xc = _silu(conv)
    xs = xc[:, :SSD_WIDTH]
    dt = jax.nn.softplus(dt_ref[0] + dtb_ref[...])
    d_a = jnp.exp(dt * acoef_ref[...])
    xdt = xs * dt
    heads_per_chunk = LANES // SSD_HEAD_DIM
    chunks = range(SSD_WIDTH // LANES)
    x_cols = [_column(xdt[:, c * LANES:(c + 1) * LANES]) for c in chunks]
    da_cols = [_column(d_a[:, c * LANES:(c + 1) * LANES]) for c in chunks]
    y_cols = []
    for c in chunks:
        parts = []
        for hh in range(heads_per_chunk):
            h = c * heads_per_chunk + hh
            g = h // HEADS_PER_GROUP
            rows = slice(hh * SSD_HEAD_DIM, (hh + 1) * SSD_HEAD_DIM)
            b_g = xc[:, SSD_WIDTH + g * D_STATE:SSD_WIDTH + (g + 1) * D_STATE]
            c_g = xc[:, SSD_WIDTH + (SSD_GROUPS + g) * D_STATE:SSD_WIDTH + (SSD_GROUPS + g + 1) * D_STATE]
            st = ss_ref[0, h] * da_cols[c][rows, :] + x_cols[c][rows, :] * b_g
            ns_ref[0, h] = st
            parts.append(jnp.sum(st * c_g, axis=1, keepdims=True))
        y_cols.append(jnp.concatenate(parts, axis=0))
    for c in chunks:
        y_ref[:, c * LANES:(c + 1) * LANES] = jnp.broadcast_to(y_cols[c], (LANES, LANES)).T[0:1, :]
    y = (y_ref[...] + dskip_ref[...] * xs) * _silu(z_ref[0])
    o_ref[0] = _rmsnorm(y, ng_ref[...]).astype(o_ref.dtype)


def _ssd_sample(xbc3, z3, dte3, state_conv, state_ssm, layer, conv_w, conv_b, dt_bias, a_log, d_skip, ssd_norm_g):
    n_seq = xbc3.shape[0]
    per_channel = lambda v: jnp.repeat(v.astype(F32), SSD_HEAD_DIM).reshape(1, SSD_WIDTH)
    seq3 = lambda w: pl.BlockSpec((1, 1, w), lambda b: (b, 0, 0))
    state_spec = pl.BlockSpec((None, 1, N_SSD_HEADS, SSD_HEAD_DIM, D_STATE), lambda b: (layer, b, 0, 0, 0))
    in_specs = [seq3(CONV_DIM), seq3(SSD_WIDTH), seq3(SSD_WIDTH),
                pl.BlockSpec((None, 1, CONV_WIDTH - 1, CONV_DIM), lambda b: (layer, b, 0, 0)), state_spec,
                _resident((CONV_WIDTH, CONV_DIM)), _resident((1, CONV_DIM)), _resident((1, SSD_WIDTH)),
                _resident((1, SSD_WIDTH)), _resident((1, SSD_WIDTH)), _resident((1, SSD_WIDTH))]
    operands = [xbc3, z3, dte3, state_conv, state_ssm, conv_w, conv_b.reshape(1, CONV_DIM), per_channel(dt_bias),
                per_channel(-jnp.exp(a_log.astype(F32))), per_channel(d_skip), ssd_norm_g.reshape(1, SSD_WIDTH)]
    return pl.pallas_call(
        _ssd_sample_kernel,
        out_shape=(jax.ShapeDtypeStruct((n_seq, 1, SSD_WIDTH), BF16),
                   jax.ShapeDtypeStruct((n_seq, CONV_WIDTH - 1, CONV_DIM), F32),
                   jax.ShapeDtypeStruct((n_seq, N_SSD_HEADS, SSD_HEAD_DIM, D_STATE), F32)),
        grid=(n_seq,), in_specs=in_specs,
        out_specs=(seq3(SSD_WIDTH), pl.BlockSpec((1, CONV_WIDTH - 1, CONV_DIM), lambda b: (b, 0, 0)),
                   pl.BlockSpec((1, N_SSD_HEADS, SSD_HEAD_DIM, D_STATE), lambda b: (b, 0, 0, 0))),
        scratch_shapes=[pltpu.VMEM((1, SSD_WIDTH), F32)],
        compiler_params=_cparams("parallel"), name="ssd_sample",
    )(*operands)


def _sample_layer(x, layer, cache_k, cache_v, state_conv, state_ssm, page_table,
                  norm_g, w_in, conv_w, conv_b, dt_bias, a_log, d_skip, ssd_norm_g, w_out, final_g, final_norm):
    n_seq = x.shape[0]
    u = _in_proj_sample(x, norm_g, w_in)
    q, k, v, g_att = u[:, :Q_END], u[:, Q_END:K_END], u[:, K_END:V_END], u[:, V_END:G_END]
    z, xbc, dte = u[:, G_END:Z_END], u[:, Z_END:XBC_END], u[:, XBC_END:]
    cache_k_pages = cache_k.reshape(cache_k.shape[0], cache_k.shape[1], PAGE_ROWS, HEAD_DIM)
    kbar = _kbar_sample(cache_k_pages, page_table, layer)
    idx = _gate_sample(q.reshape(n_seq * N_ATT_HEADS, HEAD_DIM), kbar)
    heads = lambda a, r: a.reshape(n_seq, r, HEAD_DIM)
    o_att = _attn_sample(heads(q, N_ATT_HEADS), heads(k, N_KV_HEADS), heads(v, N_KV_HEADS),
                         heads(g_att, N_ATT_HEADS), cache_k, cache_v, page_table,
                         idx[:, :MOBA_TOPK].reshape(-1), layer)
    rows3 = lambda a: a.reshape(n_seq, 1, a.shape[1])
    o_ssd, new_conv, new_ssm = _ssd_sample(rows3(xbc), rows3(z), rows3(dte), state_conv, state_ssm, layer,
                                           conv_w, conv_b, dt_bias, a_log, d_skip, ssd_norm_g)
    h = _out_proj(x, o_att.reshape(n_seq, ATT_WIDTH), o_ssd.reshape(n_seq, SSD_WIDTH), w_out.astype(BF16),
                  final_g, final_norm, n_seq)
    return h, k, v, new_conv, new_ssm


PROMPT_ROW_TILE = 512


def kernel(x_prompt, x_sample, cache_k, cache_v, state_conv, state_ssm, page_table, norm_g, w_in, conv_w,
           conv_b, dt_bias, a_log, d_skip, ssd_norm_g, w_out, final_norm_g):
    depth = w_in.shape[0]
    batch, seq, _ = x_prompt.shape
    n_seq, dec_seq, _ = x_sample.shape
    assert batch == 1 and dec_seq == 1
    hp, hs = x_prompt[0], x_sample[:, 0]
    outs = [[] for _ in range(8)]
    for l in range(depth):
        last = l == depth - 1
        params = (norm_g[l], _in_proj_weights(w_in[l]), conv_w[l], conv_b[l], dt_bias[l], a_log[l], d_skip[l],
                  ssd_norm_g[l], w_out[l], final_norm_g, last)
        hp, kp, vp, cp, sp = _prompt_layer(hp, *params, PROMPT_ROW_TILE)
        hs, ks, vs, cs, ss = _sample_layer(hs, l, cache_k, cache_v, state_conv, state_ssm, page_table, *params)
        kv4 = lambda a: a.reshape(n_seq, dec_seq, N_KV_HEADS, HEAD_DIM)
        for lst, val in zip(outs, (kp[None], vp[None], cp[None], sp[None], kv4(ks), kv4(vs), cs, ss)):
            lst.append(val)
    return (hp[None], hs[:, None]) + tuple(jnp.stack(o) for o in outs)
```
